```python
import numpy as np
import jax
import jax.numpy as jnp
from jax import lax

D_MODEL = 2048
BATCH = 4
SEQ = 4096
DEPTH = 1
DEC_BATCH = 32
DEC_SEQ = 4
PAST_LEN = 16384
PAGE_SIZE = 128

SB_HEAD_DIM = 128
SB_HEADS = D_MODEL // (2 * SB_HEAD_DIM)
SB_WIDTH = SB_HEADS * SB_HEAD_DIM
SB_BLOCK = 128
SB_BIAS_NEAR = -2.0
SB_BIAS_FAR = -10.0
RET_HEADS = 8
RET_QK_DIM = D_MODEL // (2 * RET_HEADS)
RET_V_DIM = D_MODEL // RET_HEADS
RET_QK_WIDTH = RET_HEADS * RET_QK_DIM
RET_V_WIDTH = RET_HEADS * RET_V_DIM
RET_CHUNK = 128
ROPE_BASE = 10000.0
IN_SIZES = (SB_WIDTH, SB_WIDTH, SB_WIDTH, RET_QK_WIDTH, RET_QK_WIDTH, RET_V_WIDTH, RET_V_WIDTH, D_MODEL, D_MODEL)
IN_WIDTH = sum(IN_SIZES)
N_EXPERTS = 32
TOP_K = 4
D_FF = D_MODEL
SWIGLU_LIMIT = 7.0
SWIGLU_ALPHA = 1.702
MOE_BLOCK = 128
N_MOD = 6
NORM_EPS = 1e-5

kernel_name = 'sb_retention_moe_adaln_decoder_step'


def rmsnorm(x, g):
    xf = x.astype(jnp.float32)
    y = xf * lax.rsqrt(jnp.mean(xf * xf, axis=-1, keepdims=True) + NORM_EPS)
    return (y * g).astype(x.dtype)


def adaln(c, w, b):
    return jnp.split(jax.nn.silu(c) @ w + b, N_MOD, axis=-1)


def modulate(h, shift, scale):
    return h * (1.0 + scale[:, None, :]) + shift[:, None, :]


def rotary(x, pos):
    half = x.shape[-1] // 2
    inv_freq = ROPE_BASE ** (-jnp.arange(half, dtype=jnp.float32) / half)
    ang = pos.astype(jnp.float32)[:, None] * inv_freq[None, :]
    cos = jnp.cos(ang)[None, :, None, :]
    sin = jnp.sin(ang)[None, :, None, :]
    x1 = x[..., :half].astype(jnp.float32)
    x2 = x[..., half:].astype(jnp.float32)
    return jnp.concatenate([x1 * cos - x2 * sin, x1 * sin + x2 * cos], axis=-1)


def split_projection(h, w_in):
    b, t = h.shape[:2]
    offsets = np.cumsum(IN_SIZES)[:-1].tolist()
    q_sb, k_sb, v_sb, q_r, k_r, v_r, g_r, a_sb, a_r = jnp.split(h @ w_in, offsets, axis=-1)
    q_sb = q_sb.reshape(b, t, SB_HEADS, SB_HEAD_DIM)
    k_sb = k_sb.reshape(b, t, SB_HEADS, SB_HEAD_DIM)
    v_sb = v_sb.reshape(b, t, SB_HEADS, SB_HEAD_DIM)
    q_r = q_r.reshape(b, t, RET_HEADS, RET_QK_DIM)
    k_r = k_r.reshape(b, t, RET_HEADS, RET_QK_DIM)
    v_r = v_r.reshape(b, t, RET_HEADS, RET_V_DIM)
    return q_sb, k_sb, v_sb, q_r, k_r, v_r, g_r, a_sb, a_r


def stick_breaking_weights(z, mask):
    log_beta = jax.nn.log_sigmoid(z)
    log_keep = jnp.where(mask, jax.nn.log_sigmoid(-z), 0.0)
    between = lax.cumsum(log_keep, axis=z.ndim - 1, reverse=True) - log_keep
    return jnp.where(mask, jnp.exp(log_beta + between), 0.0)


def sb_prompt(q, k, v, bias):
    b, s = q.shape[:2]
    nb = s // SB_BLOCK
    scale = SB_HEAD_DIM ** -0.5
    k_pos = jnp.arange(s)
    q_blocks = q.reshape(b, nb, SB_BLOCK, SB_HEADS, SB_HEAD_DIM).swapaxes(0, 1)
    bias_f = bias.astype(jnp.float32)[None, :, None, None]

    def block(args):
        qb, start = args
        q_pos = start + jnp.arange(SB_BLOCK)
        z = jnp.einsum('bqhd,bshd->bhqs', qb, k).astype(jnp.float32) * scale + bias_f
        a = stick_breaking_weights(z, k_pos[None, :] < q_pos[:, None])
        return jnp.einsum('bhqs,bshd->bqhd', a.astype(v.dtype), v)

    out = lax.map(block, (q_blocks, jnp.arange(nb) * SB_BLOCK))
    return out.swapaxes(0, 1).reshape(b, s, SB_HEADS, SB_HEAD_DIM)


def sb_sample(q, k_new, v_new, k_past, v_past, bias):
    p, t = k_past.shape[1], q.shape[1]
    scale = SB_HEAD_DIM ** -0.5
    z = jnp.concatenate([jnp.einsum('bqhd,bshd->bhqs', q, k_past),
                         jnp.einsum('bqhd,bshd->bhqs', q, k_new)], axis=-1).astype(jnp.float32) * scale
    z = z + bias.astype(jnp.float32)[None, :, None, None]
    k_pos = jnp.arange(p + t)
    q_pos = p + jnp.arange(t)
    a = stick_breaking_weights(z, k_pos[None, :] < q_pos[:, None]).astype(v_new.dtype)
    return (jnp.einsum('bhqs,bshd->bqhd', a[..., :p], v_past)
            + jnp.einsum('bhqs,bshd->bqhd', a[..., p:], v_new))


def retention_log_decay():
    return jnp.log1p(-jnp.exp2(-5.0 - jnp.arange(RET_HEADS, dtype=jnp.float32)))


def retention_chunk(state, q, k, v):
    q, k, v = q.astype(jnp.float32), k.astype(jnp.float32), v.astype(jnp.float32)
    c = q.shape[1]
    lg = retention_log_decay()
    n = jnp.arange(c, dtype=jnp.float32)
    diff = n[:, None] - n[None, :]
    decay = jnp.where(diff >= 0, jnp.exp(jnp.maximum(diff, 0.0) * lg[:, None, None]), 0.0)
    scores = jnp.einsum('bnhd,bmhd->bhnm', q, k) * decay[None]
    intra = jnp.einsum('bhnm,bmhe->bnhe', scores, v)
    cross = jnp.einsum('bnhd,bhde->bnhe', q, state) * jnp.exp((n[:, None] + 1.0) * lg[None, :])[None, :, :, None]
    k_dec = k * jnp.exp((c - 1.0 - n)[:, None] * lg[None, :])[None, :, :, None]
    new_state = jnp.exp(c * lg)[None, :, None, None] * state + jnp.einsum('bmhd,bmhe->bhde', k_dec, v)
    return new_state, intra + cross


def retention_prompt(q, k, v):
    b, s = q.shape[:2]
    nc = s // RET_CHUNK

    def to_chunks(t):
        return t.reshape(b, nc, RET_CHUNK, *t.shape[2:]).swapaxes(0, 1)

    state0 = jnp.zeros((b, RET_HEADS, RET_QK_DIM, RET_V_DIM), jnp.float32)
    state, o = lax.scan(lambda st, xs: retention_chunk(st, *xs), state0,
                        (to_chunks(q), to_chunks(k), to_chunks(v)))
    return state, o.swapaxes(0, 1).reshape(b, s, RET_HEADS, RET_V_DIM)


def merge_branches(o_sb, o_r, g_r, a_sb, a_r, w_up_sb, w_up_ret, w_o):
    b, t = o_sb.shape[:2]
    y_sb = o_sb.reshape(b, t, SB_WIDTH) @ w_up_sb
    of = o_r.astype(jnp.float32)
    mu = jnp.mean(of, axis=-1, keepdims=True)
    var = jnp.mean(jnp.square(of - mu), axis=-1, keepdims=True)
    o_norm = ((of - mu) * lax.rsqrt(var + NORM_EPS)).reshape(b, t, RET_V_WIDTH).astype(g_r.dtype)
    y_r = (jax.nn.silu(g_r) * o_norm) @ w_up_ret
    merged = jax.nn.sigmoid(a_sb) * y_sb + jax.nn.sigmoid(a_r) * y_r
    return merged @ w_o


def mixer_prompt(h, w_in, b_sb, w_up_sb, w_up_ret, w_o):
    q_sb, k_sb, v_sb, q_r, k_r, v_r, g_r, a_sb, a_r = split_projection(h, w_in)
    pos = jnp.arange(h.shape[1])
    o_sb = sb_prompt(q_sb, k_sb, v_sb, b_sb)
    state, o_r = retention_prompt(rotary(q_r, pos), rotary(k_r, pos) * RET_QK_DIM ** -0.5, v_r)
    y = merge_branches(o_sb, o_r, g_r, a_sb, a_r, w_up_sb, w_up_ret, w_o)
    return y, k_sb, v_sb, state


def mixer_sample(h, k_past, v_past, state, w_in, b_sb, w_up_sb, w_up_ret, w_o):
    q_sb, k_sb, v_sb, q_r, k_r, v_r, g_r, a_sb, a_r = split_projection(h, w_in)
    pos = k_past.shape[1] + jnp.arange(h.shape[1])
    o_sb = sb_sample(q_sb, k_sb, v_sb, k_past, v_past, b_sb)
    new_state, o_r = retention_chunk(state.astype(jnp.float32), rotary(q_r, pos),
                                     rotary(k_r, pos) * RET_QK_DIM ** -0.5, v_r)
    y = merge_branches(o_sb, o_r, g_r, a_sb, a_r, w_up_sb, w_up_ret, w_o)
    return y, k_sb, v_sb, new_state


def routed_moe(h, w_router, b_router, w_gate_up, b_gate_up, w_down, b_down, layer):
    lead = h.shape[:-1]
    xt = h.reshape(-1, D_MODEL)
    t = xt.shape[0]
    logits = (xt @ w_router).astype(jnp.float32) + b_router
    top_logit, top_e = lax.top_k(logits, TOP_K)
    top_w = jax.nn.softmax(top_logit, axis=-1)
    n_assign = t * TOP_K
    flat_e = top_e.reshape(n_assign)
    flat_w = top_w.reshape(n_assign)
    flat_tok = jnp.arange(n_assign, dtype=jnp.int32) // TOP_K
    order = jnp.argsort(flat_e)
    se, stok, sw = flat_e[order], flat_tok[order], flat_w[order]
    counts = jnp.zeros((N_EXPERTS,), jnp.int32).at[flat_e].add(1)
    starts = jnp.cumsum(counts) - counts
    pcounts = (counts + MOE_BLOCK - 1) // MOE_BLOCK * MOE_BLOCK
    pends = jnp.cumsum(pcounts)
    pstarts = pends - pcounts
    dest = pstarts[se] + jnp.arange(n_assign, dtype=jnp.int32) - starts[se]
    n_blk = -(-n_assign // MOE_BLOCK) + N_EXPERTS
    n_slot = n_blk * MOE_BLOCK
    slot_tok = jnp.full((n_slot,), t, jnp.int32).at[dest].set(stok)
    slot_w = jnp.zeros((n_slot,), jnp.float32).at[dest].set(sw)
    blk_e = jnp.minimum(jnp.searchsorted(pends, jnp.arange(n_blk, dtype=jnp.int32) * MOE_BLOCK, side='right'),
                        N_EXPERTS - 1)
    x_pad = jnp.concatenate([xt, jnp.zeros((1, D_MODEL), xt.dtype)], axis=0)
    xb = x_pad[slot_tok].reshape(n_blk, MOE_BLOCK, D_MODEL)

    def expert_block(args):
        xblk, e = args
        gu = xblk @ w_gate_up[layer, e] + b_gate_up[layer, e]
        gate = jnp.minimum(gu[:, :D_FF], SWIGLU_LIMIT)
        up = jnp.clip(gu[:, D_FF:], -SWIGLU_LIMIT, SWIGLU_LIMIT)
        act = (up + 1.0) * gate * jax.nn.sigmoid(SWIGLU_ALPHA * gate)
        return act @ w_down[layer, e] + b_down[layer, e]

    yb = lax.map(expert_block, (xb, blk_e)).reshape(n_slot, D_MODEL)
    y = jnp.zeros((t + 1, D_MODEL), jnp.float32).at[slot_tok].add(yb.astype(jnp.float32) * slot_w[:, None])
    return y[:t].reshape(*lead, D_MODEL).astype(h.dtype)


def setup_inputs(seed: int = 0) -> dict:
    key = jax.random.key(seed)
    ks = jax.random.split(key, 24)
    f32 = jnp.float32
    n_pages = PAST_LEN // PAGE_SIZE
    n_used = DEC_BATCH * n_pages
    n_phys = n_used + max(1, n_used // 4)

    def dense(k, shape, fan_in):
        return jax.random.normal(k, shape, f32) * fan_in ** -0.5

    def small(k, shape):
        return 0.02 * jax.random.normal(k, shape, f32)

    def gain(k, shape):
        return 1.0 + 0.05 * jax.random.normal(k, shape, f32)

    page_table = jax.random.permutation(ks[5], n_phys)[:n_used].reshape(DEC_BATCH, n_pages).astype(jnp.int32)
    sb_bias0 = jnp.linspace(SB_BIAS_NEAR, SB_BIAS_FAR, SB_HEADS, dtype=f32)
    return {
        'x_prompt': jax.random.normal(ks[0], (BATCH, SEQ, D_MODEL), f32),
        'x_sample': jax.random.normal(ks[1], (DEC_BATCH, DEC_SEQ, D_MODEL), f32),
        'cache_sb_k': jax.random.normal(ks[2], (DEPTH, n_phys, PAGE_SIZE, SB_HEADS, SB_HEAD_DIM), f32),
        'cache_sb_v': jax.random.normal(ks[3], (DEPTH, n_phys, PAGE_SIZE, SB_HEADS, SB_HEAD_DIM), f32),
        'state_ret': jax.random.normal(ks[4], (DEPTH, DEC_BATCH, RET_HEADS, RET_QK_DIM, RET_V_DIM), f32),
        'page_table': page_table,
        'c_prompt': jax.random.normal(ks[6], (BATCH, D_MODEL), f32),
        'c_sample': jax.random.normal(ks[7], (DEC_BATCH, D_MODEL), f32),
        'w_ada': dense(ks[8], (DEPTH, D_MODEL, N_MOD * D_MODEL), D_MODEL),
        'b_ada': small(ks[9], (DEPTH, N_MOD * D_MODEL)),
        'g_mix': gain(ks[10], (DEPTH, D_MODEL)),
        'w_in': dense(ks[11], (DEPTH, D_MODEL, IN_WIDTH), D_MODEL),
        'b_sb': sb_bias0[None, :] + 0.1 * jax.random.normal(ks[23], (DEPTH, SB_HEADS), f32),
        'w_up_sb': dense(ks[12], (DEPTH, SB_WIDTH, D_MODEL), SB_WIDTH),
        'w_up_ret': dense(ks[13], (DEPTH, RET_V_WIDTH, D_MODEL), RET_V_WIDTH),
        'w_o': dense(ks[14], (DEPTH, D_MODEL, D_MODEL), D_MODEL),
        'g_ffn': gain(ks[15], (DEPTH, D_MODEL)),
        'w_router': dense(ks[16], (DEPTH, D_MODEL, N_EXPERTS), D_MODEL),
        'b_router': 0.01 * jax.random.normal(ks[17], (DEPTH, N_EXPERTS), f32),
        'w_gate_up': dense(ks[18], (DEPTH, N_EXPERTS, D_MODEL, 2 * D_FF), D_MODEL),
        'b_gate_up': small(ks[19], (DEPTH, N_EXPERTS, 2 * D_FF)),
        'w_down': dense(ks[20], (DEPTH, N_EXPERTS, D_FF, D_MODEL), D_FF),
        'b_down': small(ks[21], (DEPTH, N_EXPERTS, D_MODEL)),
        'g_final': gain(ks[22], (D_MODEL,)),
    }


def reference(x_prompt, x_sample, cache_sb_k, cache_sb_v, state_ret, page_table, c_prompt, c_sample,
              w_ada, b_ada, g_mix, w_in, b_sb, w_up_sb, w_up_ret, w_o, g_ffn, w_router, b_router,
              w_gate_up, b_gate_up, w_down, b_down, g_final):
    n_dec, n_pages = page_table.shape
    past = n_pages * PAGE_SIZE
    xp, xs = x_prompt, x_sample
    kp_list, vp_list, sp_list, ks_list, vs_list, ss_list = [], [], [], [], [], []
    for l in range(DEPTH):
        sh_mp, sc_mp, gt_mp, sh_fp, sc_fp, gt_fp = adaln(c_prompt, w_ada[l], b_ada[l])
        sh_ms, sc_ms, gt_ms, sh_fs, sc_fs, gt_fs = adaln(c_sample, w_ada[l], b_ada[l])

        h = modulate(rmsnorm(xp, g_mix[l]), sh_mp, sc_mp)
        mix, k_new, v_new, s_new = mixer_prompt(h, w_in[l], b_sb[l], w_up_sb[l], w_up_ret[l], w_o[l])
        xp = xp + gt_mp[:, None, :] * mix
        kp_list.append(k_new)
        vp_list.append(v_new)
        sp_list.append(s_new.astype(state_ret.dtype))
        h = modulate(rmsnorm(xp, g_ffn[l]), sh_fp, sc_fp)
        xp = xp + gt_fp[:, None, :] * routed_moe(h, w_router[l], b_router[l], w_gate_up, b_gate_up,
                                                   w_down, b_down, l)

        k_past = cache_sb_k[l][page_table].reshape(n_dec, past, SB_HEADS, SB_HEAD_DIM)
        v_past = cache_sb_v[l][page_table].reshape(n_dec, past, SB_HEADS, SB_HEAD_DIM)
        h = modulate(rmsnorm(xs, g_mix[l]), sh_ms, sc_ms)
        mix, k_new, v_new, s_new = mixer_sample(h, k_past, v_past, state_ret[l],
                                                w_in[l], b_sb[l], w_up_sb[l], w_up_ret[l], w_o[l])
        xs = xs + gt_ms[:, None, :] * mix
        ks_list.append(k_new)
        vs_list.append(v_new)
        ss_list.append(s_new.astype(state_ret.dtype))
        h = modulate(rmsnorm(xs, g_ffn[l]), sh_fs, sc_fs)
        xs = xs + gt_fs[:, None, :] * routed_moe(h, w_router[l], b_router[l], w_gate_up, b_gate_up,
                                                   w_down, b_down, l)

    y_prompt = rmsnorm(xp, g_final)
    y_sample = rmsnorm(xs, g_final)
    return (y_prompt, y_sample, jnp.stack(kp_list), jnp.stack(vp_list), jnp.stack(sp_list),
            jnp.stack(ks_list), jnp.stack(vs_list), jnp.stack(ss_list))
```

```python
import functools

import jax
import jax.numpy as jnp
from jax import lax
from jax.experimental import pallas as pl
from jax.experimental.pallas import tpu as pltpu

BF16 = jnp.bfloat16
F32 = jnp.float32
I32 = jnp.int32

NORM_EPS = 1e-5
ROPE_BASE = 10000.0
N_MOD = 6
TOP_K = 4
DEST_TILE = 128
SWIGLU_LIMIT = 7.0
SWIGLU_ALPHA = 1.702

V7X_VMEM_BYTES = 64 * 1024 * 1024
V7X_LANES = 128
V7X_SUBLANES = 8
VMEM_CEILING = V7X_VMEM_BYTES * 7 // 8


def _cparams(semantics, vmem_estimate):
    limit = min(max(int(vmem_estimate), 16 * 1024 * 1024), VMEM_CEILING)
    return pltpu.CompilerParams(dimension_semantics=semantics, vmem_limit_bytes=limit)


def _nbytes(shape, dtype):
    n = 1
    for s in shape:
        n *= s
    return n * jnp.dtype(dtype).itemsize


def _tile(n, want):
    t = min(n, want)
    while n % t:
        t -= 1
    return t


def _dot(a, b):
    return jnp.dot(a, b, preferred_element_type=F32)


def _dot_nt(a, b):
    return lax.dot_general(a, b, (((1,), (1,)), ((), ())), preferred_element_type=F32)


def _dot_tn(a, b):
    return lax.dot_general(a, b, (((0,), (0,)), ((), ())), preferred_element_type=F32)


def _sigmoid(x):
    return 1.0 / (1.0 + jnp.exp(-x))


def _neg_softplus(z):
    return -(jnp.maximum(z, 0.0) + jnp.log1p(jnp.exp(-jnp.abs(z))))


def _tri_ones(n):
    row = lax.broadcasted_iota(I32, (n, n + V7X_LANES), 0)
    col = lax.broadcasted_iota(I32, (n, n + V7X_LANES), 1)
    return jnp.where(jnp.logical_or(col >= n, row > col), 1.0, 0.0).astype(BF16)


def _ada_kernel(c_ref, w_ref, b_ref, o_ref):
    c = c_ref[...]
    s = (c * _sigmoid(c)).astype(BF16)
    o_ref[...] = _dot(s, w_ref[...].astype(BF16)) + b_ref[...]


def _adaln(c, w, b):
    m, d = c.shape
    n = w.shape[1]
    tn = _tile(n, 1024)
    est = 2 * (_nbytes((d, tn), F32) + _nbytes((m, tn), F32)) + _nbytes((d, tn), BF16) + _nbytes((m, d), F32) * 4
    return pl.pallas_call(
        _ada_kernel,
        grid=(n // tn,),
        in_specs=[pl.BlockSpec((m, d), lambda j: (0, 0)),
                  pl.BlockSpec((d, tn), lambda j: (0, j)),
                  pl.BlockSpec((1, tn), lambda j: (0, j))],
        out_specs=pl.BlockSpec((m, tn), lambda j: (0, j)),
        out_shape=jax.ShapeDtypeStruct((m, n), F32),
        compiler_params=_cparams(("parallel",), est + (8 << 20)),
        name="adaln",
    )(c, w, b.reshape(1, n))


def _normed(x_ref, g_ref, sh_ref, sc_ref):
    x = x_ref[0]
    y = x * lax.rsqrt(jnp.mean(x * x, axis=-1, keepdims=True) + NORM_EPS) * g_ref[...]
    return y * (1.0 + sc_ref[0]) + sh_ref[0]


def _norm_mod_kernel(x_ref, g_ref, sh_ref, sc_ref, h_ref):
    h_ref[0] = _normed(x_ref, g_ref, sh_ref, sc_ref).astype(h_ref.dtype)


def _norm_router_kernel(x_ref, g_ref, sh_ref, sc_ref, wr_ref, br_ref, h_ref, e_ref, p_ref):
    h = _normed(x_ref, g_ref, sh_ref, sc_ref)
    h_ref[0] = h
    logits = _dot_nt(wr_ref[...], h.astype(BF16)) + br_ref[...]
    n_exp = logits.shape[0]
    iota_e = lax.broadcasted_iota(I32, logits.shape, 0)
    tops = []
    for k in range(TOP_K):
        m = jnp.max(logits, axis=0, keepdims=True)
        idx = jnp.min(jnp.where(logits == m, iota_e, n_exp), axis=0, keepdims=True)
        e_ref[0, k:k + 1, :] = idx
        tops.append(m)
        logits = jnp.where(iota_e == idx, -jnp.inf, logits)
    ex = [jnp.exp(t - tops[0]) for t in tops]
    inv = 1.0 / (ex[0] + ex[1] + ex[2] + ex[3])
    for k in range(TOP_K):
        p_ref[0, k:k + 1, :] = ex[k] * inv


def _mod_spec(mod, ts, d):
    if mod.shape[1] == 1:
        return pl.BlockSpec((1, 1, d), lambda b, i: (b, 0, 0))
    return pl.BlockSpec((1, ts, d), lambda b, i: (b, i, 0))


def _norm_mod(x, g, sh, sc):
    nb, s, d = x.shape
    ts = _tile(s, 512)
    est = 2 * (_nbytes((ts, d), F32) + _nbytes((ts, d), BF16)) + 4 * _nbytes((ts, d), F32)
    return pl.pallas_call(
        _norm_mod_kernel,
        grid=(nb, s // ts),
        in_specs=[pl.BlockSpec((1, ts, d), lambda b, i: (b, i, 0)),
                  pl.BlockSpec((1, d), lambda b, i: (0, 0)),
                  _mod_spec(sh, ts, d), _mod_spec(sc, ts, d)],
        out_specs=pl.BlockSpec((1, ts, d), lambda b, i: (b, i, 0)),
        out_shape=jax.ShapeDtypeStruct((nb, s, d), BF16),
        compiler_params=_cparams(("parallel", "parallel"), est + (8 << 20)),
        name="norm_mod",
    )(x, g.reshape(1, d), sh, sc)


def _norm_router(x, g, sh, sc, w_router_t, b_router):
    nb, s, d = x.shape
    n_exp = w_router_t.shape[0]
    ts = _tile(s, 512)
    est = 4 * _nbytes((ts, d), F32) + 6 * _nbytes((ts, d), F32)
    return pl.pallas_call(
        _norm_router_kernel,
        grid=(nb, s // ts),
        in_specs=[pl.BlockSpec((1, ts, d), lambda b, i: (b, i, 0)),
                  pl.BlockSpec((1, d), lambda b, i: (0, 0)),
                  _mod_spec(sh, ts, d), _mod_spec(sc, ts, d),
                  pl.BlockSpec((n_exp, d), lambda b, i: (0, 0)),
                  pl.BlockSpec((n_exp, 1), lambda b, i: (0, 0))],
        out_specs=[pl.BlockSpec((1, ts, d), lambda b, i: (b, i, 0)),
                   pl.BlockSpec((1, TOP_K, ts), lambda b, i: (b, 0, i)),
                   pl.BlockSpec((1, TOP_K, ts), lambda b, i: (b, 0, i))],
        out_shape=[jax.ShapeDtypeStruct((nb, s, d), F32),
                   jax.ShapeDtypeStruct((nb, TOP_K, s), I32),
                   jax.ShapeDtypeStruct((nb, TOP_K, s), F32)],
        compiler_params=_cparams(("parallel", "parallel"), est + (8 << 20)),
        name="norm_router",
    )(x, g.reshape(1, d), sh, sc, w_router_t, b_router.reshape(n_exp, 1))


def _mm_kernel(a_ref, w_ref, o_ref):
    o_ref[0] = _dot(a_ref[0], w_ref[...]).astype(o_ref.dtype)


def _mm_res_kernel(a_ref, w_ref, res_ref, gate_ref, o_ref):
    o_ref[0] = res_ref[0] + gate_ref[0] * _dot(a_ref[0], w_ref[...])


def _mm_tiles(s, k, n):
    return _tile(s, 1024), _tile(n, 1024)


def _matmul(a, w, out_dtype):
    nb, s, k = a.shape
    n = w.shape[1]
    tm, tn = _mm_tiles(s, k, n)
    est = 2 * (_nbytes((tm, k), BF16) + _nbytes((k, tn), BF16) + _nbytes((tm, tn), out_dtype)) + _nbytes((tm, tn), F32)
    return pl.pallas_call(
        _mm_kernel,
        grid=(nb, s // tm, n // tn),
        in_specs=[pl.BlockSpec((1, tm, k), lambda b, i, j: (b, i, 0)),
                  pl.BlockSpec((k, tn), lambda b, i, j: (0, j))],
        out_specs=pl.BlockSpec((1, tm, tn), lambda b, i, j: (b, i, j)),
        out_shape=jax.ShapeDtypeStruct((nb, s, n), out_dtype),
        compiler_params=_cparams(("parallel", "parallel", "parallel"), est + (8 << 20)),
        name="matmul",
    )(a, w)


def _matmul_residual(a, w, res, gate):
    nb, s, k = a.shape
    n = w.shape[1]
    tm, tn = _mm_tiles(s, k, n)
    if gate.shape[1] == 1:
        gate_spec = pl.BlockSpec((1, 1, tn), lambda b, i, j: (b, 0, j))
    else:
        gate_spec = pl.BlockSpec((1, tm, tn), lambda b, i, j: (b, i, j))
    est = 2 * (_nbytes((tm, k), BF16) + _nbytes((k, tn), BF16) + 3 * _nbytes((tm, tn), F32)) + _nbytes((tm, tn), F32)
    return pl.pallas_call(
        _mm_res_kernel,
        grid=(nb, s // tm, n // tn),
        in_specs=[pl.BlockSpec((1, tm, k), lambda b, i, j: (b, i, 0)),
                  pl.BlockSpec((k, tn), lambda b, i, j: (0, j)),
                  pl.BlockSpec((1, tm, tn), lambda b, i, j: (b, i, j)),
                  gate_spec],
        out_specs=pl.BlockSpec((1, tm, tn), lambda b, i, j: (b, i, j)),
        out_shape=jax.ShapeDtypeStruct((nb, s, n), F32),
        compiler_params=_cparams(("parallel", "parallel", "parallel"), est + (8 << 20)),
        name="matmul_residual",
    )(a, w, res, gate)


def _sb_block(q, k, v, bias, scale, carry, tri_ones, mask):
    bk = k.shape[0]
    z = _dot_nt(q, k) * scale + bias
    lk = _neg_softplus(z)
    lb = z + lk
    if mask is not None:
        lk = jnp.where(mask, lk, 0.0)
    cs = _dot(lk.astype(BF16), tri_ones)
    between = cs[:, :bk] + jnp.concatenate([carry] * (bk // V7X_LANES), axis=1)
    a = jnp.exp(lb + between)
    if mask is not None:
        a = jnp.where(mask, a, 0.0)
    return _dot(a.astype(BF16), v), carry + cs[:, bk:]


def _sbp_kernel(bias_ref, q_ref, k_ref, v_ref, o_ref, kb_ref, vb_ref, acc_ref, carry_ref, *, blk, scale):
    h = pl.program_id(1)
    qi = pl.program_id(2)

    @pl.when(qi == 0)
    def _():
        kb_ref[...] = k_ref[0].astype(BF16)
        vb_ref[...] = v_ref[0].astype(BF16)

    bias = bias_ref[h]
    q = q_ref[0]
    tri_ones = _tri_ones(blk)
    row = lax.broadcasted_iota(I32, (blk, blk), 0)
    col = lax.broadcasted_iota(I32, (blk, blk), 1)
    start = pl.multiple_of(qi * blk, blk)
    out, carry = _sb_block(q, kb_ref[pl.ds(start, blk), :], vb_ref[pl.ds(start, blk), :], bias, scale,
                           jnp.zeros((blk, V7X_LANES), F32), tri_ones, col < row)
    acc_ref[...] = out
    carry_ref[...] = carry

    def body(j, _):
        s0 = pl.multiple_of((qi - 1 - j) * blk, blk)
        out, carry = _sb_block(q, kb_ref[pl.ds(s0, blk), :], vb_ref[pl.ds(s0, blk), :], bias, scale,
                               carry_ref[...], tri_ones, None)
        acc_ref[...] += out
        carry_ref[...] = carry
        return 0

    lax.fori_loop(0, qi, body, 0)
    o_ref[0] = acc_ref[...].astype(o_ref.dtype)


def _sb_prompt(q, k, v, bias, n_heads):
    nb, s, w = q.shape
    dh = w // n_heads
    blk = _tile(s, 256)
    est = 4 * _nbytes((s, dh), F32) + 2 * _nbytes((s, dh), BF16) + 16 * _nbytes((blk, blk), F32)
    return pl.pallas_call(
        functools.partial(_sbp_kernel, blk=blk, scale=dh ** -0.5),
        grid_spec=pltpu.PrefetchScalarGridSpec(
            num_scalar_prefetch=0,
            grid=(nb, n_heads, s // blk),
            in_specs=[pl.BlockSpec(memory_space=pltpu.SMEM),
                      pl.BlockSpec((1, blk, dh), lambda b, h, i: (b, i, h)),
                      pl.BlockSpec((1, s, dh), lambda b, h, i: (b, 0, h)),
                      pl.BlockSpec((1, s, dh), lambda b, h, i: (b, 0, h))],
            out_specs=pl.BlockSpec((1, blk, dh), lambda b, h, i: (b, i, h)),
            scratch_shapes=[pltpu.VMEM((s, dh), BF16), pltpu.VMEM((s, dh), BF16),
                            pltpu.VMEM((blk, dh), F32), pltpu.VMEM((blk, V7X_LANES), F32)]),
        out_shape=jax.ShapeDtypeStruct((nb, s, w), BF16),
        compiler_params=_cparams(("parallel", "parallel", "arbitrary"), est + (8 << 20)),
        name="sb_prompt",
    )(bias, q, k, v)


def _sbs_kernel(pt_ref, bias_ref, q_ref, kn_ref, vn_ref, *refs, pages_per_step, rows_per_head, n_heads, scale):
    del pt_ref
    k_refs = refs[:pages_per_step]
    v_refs = refs[pages_per_step:2 * pages_per_step]
    o_ref, acc_ref, carry_ref = refs[2 * pages_per_step:]
    s = pl.program_id(1)
    q = q_ref[0]
    bias = bias_ref[...]
    page = kn_ref.shape[1]
    dh = q.shape[1] // n_heads
    tri_ones = _tri_ones(page)

    def sweep(k, v, mask):
        out, carry = _sb_block(q, k.astype(BF16), v.astype(BF16), bias, scale, carry_ref[...], tri_ones, mask)
        acc_ref[...] += out
        carry_ref[...] = carry

    @pl.when(s == 0)
    def _():
        acc_ref[...] = jnp.zeros_like(acc_ref)
        carry_ref[...] = jnp.zeros_like(carry_ref)
        row = lax.broadcasted_iota(I32, (q.shape[0], page), 0)
        col = lax.broadcasted_iota(I32, (q.shape[0], page), 1)
        sweep(kn_ref[0], vn_ref[0], col < row % rows_per_head)

    for p in range(pages_per_step):
        sweep(k_refs[p][0], v_refs[p][0], None)

    @pl.when(s == pl.num_programs(1) - 1)
    def _():
        for h in range(n_heads):
            rows = slice(h * rows_per_head, (h + 1) * rows_per_head)
            cols = slice(h * dh, (h + 1) * dh)
            o_ref[0, :, cols] = acc_ref[rows, cols]


def _sb_sample(q_bd, k_new, v_new, cache_k, cache_v, page_table, bias_rows, n_heads, rows_per_head):
    nbd, r, w = q_bd.shape
    page = cache_k.shape[1]
    n_pages = page_table.shape[1]
    pps = _tile(n_pages, 4)
    n_steps = n_pages // pps

    def page_spec(p):
        return pl.BlockSpec((1, page, w), lambda b, s, pt: (pt[b, n_pages - 1 - (s * pps + p)], 0, 0))

    in_specs = [pl.BlockSpec((r, V7X_LANES), lambda b, s, pt: (0, 0)),
                pl.BlockSpec((1, r, w), lambda b, s, pt: (b, 0, 0)),
                pl.BlockSpec((1, page, w), lambda b, s, pt: (b, 0, 0)),
                pl.BlockSpec((1, page, w), lambda b, s, pt: (b, 0, 0))]
    in_specs += [page_spec(p) for p in range(pps)] * 2
    est = 2 * (2 * pps + 2) * _nbytes((page, w), F32) + 8 * _nbytes((r, w), F32) + 4 * _nbytes((page, w), BF16)
    return pl.pallas_call(
        functools.partial(_sbs_kernel, pages_per_step=pps, rows_per_head=rows_per_head, n_heads=n_heads,
                          scale=(w // n_heads) ** -0.5),
        grid_spec=pltpu.PrefetchScalarGridSpec(
            num_scalar_prefetch=1,
            grid=(nbd, n_steps),
            in_specs=in_specs,
            out_specs=pl.BlockSpec((1, rows_per_head, w), lambda b, s, pt: (b, 0, 0)),
            scratch_shapes=[pltpu.VMEM((r, w), F32), pltpu.VMEM((r, V7X_LANES), F32)]),
        out_shape=jax.ShapeDtypeStruct((nbd, rows_per_head, w), F32),
        compiler_params=_cparams(("parallel", "arbitrary"), est + (8 << 20)),
        name="sb_sample",
    )(page_table, bias_rows, q_bd, k_new, v_new, *([cache_k] * pps), *([cache_v] * pps))


def _ret_kernel(lg_ref, q_ref, k_ref, v_ref, cos_ref, sin_ref, *refs, c_true, has_init, kscale):
    if has_init:
        s0_ref, o_ref, sout_ref, st_ref = refs
    else:
        o_ref, sout_ref, st_ref = refs
    h = pl.program_id(1)
    c = pl.program_id(2)
    lg = lg_ref[h]

    @pl.when(c == 0)
    def _():
        if has_init:
            st_ref[...] = s0_ref[0, 0]
        else:
            st_ref[...] = jnp.zeros_like(st_ref)

    cos = cos_ref[...]
    sin = sin_ref[...]
    n_rows, dk = cos.shape

    def rot(x):
        return x * cos + pltpu.roll(x, dk // 2, 1) * sin

    q = rot(q_ref[0])
    k = rot(k_ref[0]) * kscale
    v = v_ref[0]
    n = lax.broadcasted_iota(I32, (n_rows, n_rows), 0)
    m = lax.broadcasted_iota(I32, (n_rows, n_rows), 1)
    diff = (n - m).astype(F32)
    decay = jnp.where(diff >= 0.0, jnp.exp(jnp.maximum(diff, 0.0) * lg), 0.0)
    scores = _dot_nt(q.astype(BF16), k.astype(BF16)) * decay
    pos = lax.broadcasted_iota(I32, (n_rows, dk), 0).astype(F32)
    q_dec = q * jnp.exp((pos + 1.0) * lg)
    k_dec = k * jnp.exp((c_true - 1.0 - pos) * lg)
    st = st_ref[...]
    o = _dot(scores.astype(BF16), v) + _dot(q_dec.astype(BF16), st.astype(BF16))
    st_new = jnp.exp(jnp.full((1, st.shape[1]), c_true * lg, F32)) * st + _dot_tn(k_dec.astype(BF16), v)
    st_ref[...] = st_new
    mu = jnp.mean(o, axis=-1, keepdims=True)
    dev = o - mu
    var = jnp.mean(dev * dev, axis=-1, keepdims=True)
    o_ref[0] = (dev * lax.rsqrt(var + NORM_EPS)).astype(o_ref.dtype)

    @pl.when(c == pl.num_programs(2) - 1)
    def _():
        sout_ref[0, 0] = st_new


def _retention(qk, v_src, cos2, sin2, log_decay, state0, n_heads, dk, dv, chunk, c_true):
    nb, s, _ = qk.shape
    nc = s // chunk
    in_specs = [pl.BlockSpec(memory_space=pltpu.SMEM),
                pl.BlockSpec((1, chunk, dk), lambda b, h, c: (b, c, h)),
                pl.BlockSpec((1, chunk, dk), lambda b, h, c: (b, c, n_heads + h)),
                pl.BlockSpec((1, chunk, dv), lambda b, h, c: (b, c, h)),
                pl.BlockSpec((chunk, dk), lambda b, h, c: (c, 0)),
                pl.BlockSpec((chunk, dk), lambda b, h, c: (c, 0))]
    args = [log_decay, qk, qk, v_src, cos2, sin2]
    if state0 is not None:
        in_specs.append(pl.BlockSpec((1, 1, dk, dv), lambda b, h, c: (b, h, 0, 0)))
        args.append(state0)
    est = 8 * _nbytes((dk, dv), F32) + 16 * _nbytes((chunk, dv), F32) + 8 * _nbytes((chunk, chunk), F32)
    return pl.pallas_call(
        functools.partial(_ret_kernel, c_true=float(c_true), has_init=state0 is not None, kscale=dk ** -0.5),
        grid=(nb, n_heads, nc),
        in_specs=in_specs,
        out_specs=[pl.BlockSpec((1, chunk, dv), lambda b, h, c: (b, c, h)),
                   pl.BlockSpec((1, 1, dk, dv), lambda b, h, c: (b, h, 0, 0))],
        out_shape=[jax.ShapeDtypeStruct((nb, s, n_heads * dv), BF16),
                   jax.ShapeDtypeStruct((nb, n_heads, dk, dv), F32)],
        scratch_shapes=[pltpu.VMEM((dk, dv), F32)],
        compiler_params=_cparams(("parallel", "parallel", "arbitrary"), est + (8 << 20)),
        name="retention",
    )(*args)


def _merge_kernel(osb_ref, on_ref, g_ref, asb_ref, ar_ref, wsb_ref, wret_ref, o_ref):
    y_sb = _dot(osb_ref[0], wsb_ref[...])
    g = g_ref[0].astype(F32)
    u = (g * _sigmoid(g)) * on_ref[0].astype(F32)
    y_r = _dot(u.astype(BF16), wret_ref[...])
    merged = _sigmoid(asb_ref[0].astype(F32)) * y_sb + _sigmoid(ar_ref[0].astype(F32)) * y_r
    o_ref[0] = merged.astype(o_ref.dtype)


def _merge(o_sb, o_norm, rest, w_up_sb, w_up_ret):
    nb, s, sbw = o_sb.shape
    rvw = o_norm.shape[2]
    d = w_up_sb.shape[1]
    tm = _tile(s, 256)
    est = (2 * (_nbytes((sbw, d), BF16) + _nbytes((rvw, d), BF16))
           + 2 * (_nbytes((tm, sbw), BF16) + 5 * _nbytes((tm, d), BF16)) + 8 * _nbytes((tm, d), F32))
    return pl.pallas_call(
        _merge_kernel,
        grid=(nb, s // tm),
        in_specs=[pl.BlockSpec((1, tm, sbw), lambda b, i: (b, i, 0)),
                  pl.BlockSpec((1, tm, rvw), lambda b, i: (b, i, 0)),
                  pl.BlockSpec((1, tm, rvw), lambda b, i: (b, i, 1)),
                  pl.BlockSpec((1, tm, d), lambda b, i: (b, i, 2)),
                  pl.BlockSpec((1, tm, d), lambda b, i: (b, i, 3)),
                  pl.BlockSpec((sbw, d), lambda b, i: (0, 0)),
                  pl.BlockSpec((rvw, d), lambda b, i: (0, 0))],
        out_specs=pl.BlockSpec((1, tm, d), lambda b, i: (b, i, 0)),
        out_shape=jax.ShapeDtypeStruct((nb, s, d), BF16),
        compiler_params=_cparams(("parallel", "parallel"), est + (8 << 20)),
        name="merge",
    )(o_sb, o_norm, rest, rest, rest, w_up_sb, w_up_ret)


def _route_kernel(te_ref, dest_ref, info_ref, cnt_ref, base_ref, *, n_exp, bm, n_info):
    phase = pl.program_id(0)
    i = pl.program_id(1)
    te = te_ref[...]
    tm = te.shape[1]
    iota_e = lax.broadcasted_iota(I32, (n_exp, tm), 0)
    ones = jnp.ones((tm, V7X_LANES), BF16)
    onehots = [jnp.where(iota_e == te[k:k + 1, :], 1.0, 0.0).astype(BF16) for k in range(TOP_K)]

    @pl.when(jnp.logical_and(phase == 0, i == 0))
    def _():
        cnt_ref[...] = jnp.zeros_like(cnt_ref)

    @pl.when(phase == 0)
    def _():
        cnt_ref[...] += _dot(onehots[0] + onehots[1] + onehots[2] + onehots[3], ones)

    @pl.when(jnp.logical_and(phase == 1, i == 0))
    def _():
        counts = cnt_ref[...]
        padded = jnp.floor((counts + (bm - 1.0)) * (1.0 / bm)) * bm
        er = lax.broadcasted_iota(I32, (n_exp, n_exp), 0)
        ec = lax.broadcasted_iota(I32, (n_exp, n_exp), 1)
        lower = jnp.where(ec < er, 1.0, 0.0)
        pstart = jnp.dot(lower, padded, preferred_element_type=F32, precision=lax.Precision.HIGHEST)
        base_ref[...] = pstart
        reps = n_info // V7X_LANES
        pend = jnp.concatenate([pstart + padded] * reps, axis=1)
        used = jnp.concatenate([pstart + counts] * reps, axis=1)
        blk_start = (lax.broadcasted_iota(I32, (n_exp, n_info), 1) * bm).astype(F32)
        blk_e = jnp.minimum(jnp.sum(jnp.where(pend <= blk_start, 1.0, 0.0), axis=0, keepdims=True), n_exp - 1.0)
        mine = lax.broadcasted_iota(I32, (n_exp, n_info), 0).astype(F32) == blk_e
        valid = jnp.sum(jnp.where(mine, used - blk_start, 0.0), axis=0, keepdims=True)
        valid = jnp.clip(valid, 0.0, float(bm))
        n_active = jnp.sum(padded[:, :1], axis=0, keepdims=True) * (1.0 / bm)
        info_ref[...] = jnp.zeros_like(info_ref)
        info_ref[0:1, :] = blk_e.astype(I32)
        info_ref[1:2, :] = valid.astype(I32)
        info_ref[2:3, :] = jnp.broadcast_to(n_active, (1, n_info)).astype(I32)

    @pl.when(phase == 1)
    def _():
        tri = jnp.where(lax.broadcasted_iota(I32, (tm, tm), 0) < lax.broadcasted_iota(I32, (tm, tm), 1),
                        1.0, 0.0).astype(BF16)
        base = base_ref[...]
        for k in range(TOP_K):
            before = _dot(onehots[k], tri) + jnp.concatenate([base] * (tm // V7X_LANES), axis=1)
            dest = jnp.sum(jnp.where(onehots[k] > 0, before, 0.0), axis=0, keepdims=True)
            dest_ref[k:k + 1, :] = dest.astype(I32)
            base = base + _dot(onehots[k], ones)
        base_ref[...] = base


def _route(top_e, n_exp, bm, n_blk):
    _, tpad = top_e.shape
    tm = _tile(tpad, 512)
    n_info = -(-n_blk // V7X_LANES) * V7X_LANES
    est = 8 * _nbytes((tm, tm), F32) + 16 * _nbytes((n_exp, tm), F32)
    return pl.pallas_call(
        functools.partial(_route_kernel, n_exp=n_exp, bm=bm, n_info=n_info),
        grid=(2, tpad // tm),
        in_specs=[pl.BlockSpec((TOP_K, tm), lambda p, i: (0, i))],
        out_specs=[pl.BlockSpec((TOP_K, tm), lambda p, i: (0, i * p)),
                   pl.BlockSpec((V7X_SUBLANES, n_info), lambda p, i: (0, 0))],
        out_shape=[jax.ShapeDtypeStruct((TOP_K, tpad), I32),
                   jax.ShapeDtypeStruct((V7X_SUBLANES, n_info), I32)],
        scratch_shapes=[pltpu.VMEM((n_exp, V7X_LANES), F32), pltpu.VMEM((n_exp, V7X_LANES), F32)],
        compiler_params=_cparams(("arbitrary", "arbitrary"), est + (8 << 20)),
        name="moe_route",
    )(top_e)


def _row_copy(src_ref, src_row, dst_ref, dst_row, sem):
    return pltpu.make_async_copy(src_ref.at[pl.ds(src_row, 1)], dst_ref.at[pl.ds(dst_row, 1)], sem)


def _dispatch_kernel(dest_ref, h_ref, *refs, rows, first_tile):
    xs_ref, sem = refs[-2:]
    tile = first_tile + pl.program_id(0)

    def issue(r, _):
        for k in range(TOP_K):
            _row_copy(h_ref, r, xs_ref, dest_ref[tile, k * rows + r], sem).start()
        return 0

    lax.fori_loop(0, rows, issue, 0)

    def drain(r, _):
        for k in range(TOP_K):
            _row_copy(h_ref, 0, xs_ref, 0, sem).wait()
        return 0

    lax.fori_loop(0, rows, drain, 0)


def _dispatch(dest, h, xs, n_slot, first_tile):
    t, d = h.shape
    rows = DEST_TILE
    assert t % rows == 0
    in_specs = [pl.BlockSpec((rows, d), lambda i, dest: (i, 0))]
    args = [dest, h]
    aliases = {}
    if xs is not None:
        in_specs.append(pl.BlockSpec(memory_space=pl.ANY))
        args.append(xs)
        aliases = {2: 0}
    return pl.pallas_call(
        functools.partial(_dispatch_kernel, rows=rows, first_tile=first_tile),
        grid_spec=pltpu.PrefetchScalarGridSpec(
            num_scalar_prefetch=1,
            grid=(t // rows,),
            in_specs=in_specs,
            out_specs=pl.BlockSpec(memory_space=pl.ANY),
            scratch_shapes=[pltpu.SemaphoreType.DMA]),
        out_shape=jax.ShapeDtypeStruct((n_slot, d), F32),
        input_output_aliases=aliases,
        compiler_params=_cparams(("arbitrary",), 4 * _nbytes((rows, d), F32) + (8 << 20)),
        name="moe_dispatch",
    )(*args)


def _moe_gemm_kernel(info_ref, x_ref, wg_ref, wu_ref, bg_ref, bu_ref, wd_ref, bd_ref, o_ref, xb_ref, acc_ref):
    i = pl.program_id(0)
    j = pl.program_id(1)
    nf = pl.num_programs(1)

    @pl.when(i < info_ref[2, 0])
    def _():
        @pl.when(j == 0)
        def _():
            row = lax.broadcasted_iota(I32, x_ref.shape, 0)
            xb_ref[...] = jnp.where(row < info_ref[1, i], x_ref[...], 0.0).astype(BF16)
            acc_ref[...] = jnp.zeros_like(acc_ref)

        xb = xb_ref[...]
        gate = jnp.minimum(_dot(xb, wg_ref[0]) + bg_ref[0], SWIGLU_LIMIT)
        up = jnp.clip(_dot(xb, wu_ref[0]) + bu_ref[0], -SWIGLU_LIMIT, SWIGLU_LIMIT)
        act = (up + 1.0) * gate * _sigmoid(SWIGLU_ALPHA * gate)
        acc_ref[...] += _dot(act.astype(BF16), wd_ref[0])

        @pl.when(j == nf - 1)
        def _():
            o_ref[...] = acc_ref[...] + bd_ref[0]


def _moe_gemm(info, xs, w_gate_up, b_gate_up, w_down, b_down, bm, n_blk):
    n_slot, d = xs.shape
    n_exp, _, f2 = w_gate_up.shape
    f = f2 // 2
    tf = _tile(f, 512)
    nf = f // tf

    def blk(i, info):
        return jnp.minimum(i, info[2, 0] - 1)

    def ff(i, j, info):
        return jnp.where(i < info[2, 0], j, nf - 1)

    in_specs = [pl.BlockSpec((bm, d), lambda i, j, info: (blk(i, info), 0)),
                pl.BlockSpec((1, d, tf), lambda i, j, info: (info[0, blk(i, info)], 0, ff(i, j, info))),
                pl.BlockSpec((1, d, tf), lambda i, j, info: (info[0, blk(i, info)], 0, nf + ff(i, j, info))),
                pl.BlockSpec((1, 1, tf), lambda i, j, info: (info[0, blk(i, info)], 0, ff(i, j, info))),
                pl.BlockSpec((1, 1, tf), lambda i, j, info: (info[0, blk(i, info)], 0, nf + ff(i, j, info))),
                pl.BlockSpec((1, tf, d), lambda i, j, info: (info[0, blk(i, info)], ff(i, j, info), 0)),
                pl.BlockSpec((1, 1, d), lambda i, j, info: (info[0, blk(i, info)], 0, 0))]
    est = (2 * (2 * _nbytes((bm, d), F32) + 3 * _nbytes((d, tf), BF16)) + _nbytes((bm, d), BF16)
           + _nbytes((bm, d), F32) + 6 * _nbytes((bm, tf), F32))
    return pl.pallas_call(
        _moe_gemm_kernel,
        grid_spec=pltpu.PrefetchScalarGridSpec(
            num_scalar_prefetch=1,
            grid=(n_blk, nf),
            in_specs=in_specs,
            out_specs=pl.BlockSpec((bm, d), lambda i, j, info: (blk(i, info), 0)),
            scratch_shapes=[pltpu.VMEM((bm, d), BF16), pltpu.VMEM((bm, d), F32)]),
        out_shape=jax.ShapeDtypeStruct((n_slot, d), F32),
        compiler_params=_cparams(("arbitrary", "arbitrary"), est + (8 << 20)),
        name="moe_gemm",
    )(info, xs, w_gate_up, w_gate_up, b_gate_up.reshape(n_exp, 1, f2), b_gate_up.reshape(n_exp, 1, f2),
      w_down, b_down.reshape(n_exp, 1, d))


def _combine_kernel(dest_ref, p_ref, res_ref, gate_ref, g_ref, yb_ref, o_ref, buf_ref, sem, *, rows, first_tile,
                    final_norm):
    tile = first_tile + pl.program_id(0) * pl.num_programs(1) + pl.program_id(1)

    def issue(r, _):
        for k in range(TOP_K):
            _row_copy(yb_ref, dest_ref[tile, k * rows + r], buf_ref.at[k], r, sem).start()
        return 0

    lax.fori_loop(0, rows, issue, 0)

    def drain(r, _):
        for k in range(TOP_K):
            _row_copy(yb_ref, 0, buf_ref.at[k], 0, sem).wait()
        return 0

    lax.fori_loop(0, rows, drain, 0)
    p = p_ref[0]
    y = p[:, 0:1] * buf_ref[0]
    for k in range(1, TOP_K):
        y = y + p[:, k:k + 1] * buf_ref[k]
    x = res_ref[0] + gate_ref[0] * y
    if final_norm:
        x = x * lax.rsqrt(jnp.mean(x * x, axis=-1, keepdims=True) + NORM_EPS) * g_ref[...]
    o_ref[0] = x


def _combine(dest, probs, res, gate, g_final, yb, first_tile, final_norm):
    nb, s, d = res.shape
    rows = DEST_TILE
    assert s % rows == 0
    if gate.shape[1] == 1:
        gate_spec = pl.BlockSpec((1, 1, d), lambda b, i, dest: (b, 0, 0))
    else:
        gate_spec = pl.BlockSpec((1, rows, d), lambda b, i, dest: (b, i, 0))
    est = _nbytes((TOP_K, rows, d), F32) + 6 * _nbytes((rows, d), F32) + 4 * _nbytes((rows, d), F32)
    return pl.pallas_call(
        functools.partial(_combine_kernel, rows=rows, first_tile=first_tile, final_norm=final_norm),
        grid_spec=pltpu.PrefetchScalarGridSpec(
            num_scalar_prefetch=1,
            grid=(nb, s // rows),
            in_specs=[pl.BlockSpec((1, rows, TOP_K), lambda b, i, dest: (b, i, 0)),
                      pl.BlockSpec((1, rows, d), lambda b, i, dest: (b, i, 0)),
                      gate_spec,
                      pl.BlockSpec((1, d), lambda b, i, dest: (0, 0)),
                      pl.BlockSpec(memory_space=pl.ANY)],
            out_specs=pl.BlockSpec((1, rows, d), lambda b, i, dest: (b, i, 0)),
            scratch_shapes=[pltpu.VMEM((TOP_K, rows, d), F32), pltpu.SemaphoreType.DMA]),
        out_shape=jax.ShapeDtypeStruct((nb, s, d), F32),
        compiler_params=_cparams(("arbitrary", "arbitrary"), est + (8 << 20)),
        name="moe_combine",
    )(dest, probs, res, gate, g_final.reshape(1, d), yb)


def _rotary_tables(pos, dk):
    half = dk // 2
    inv_freq = ROPE_BASE ** (-jnp.arange(half, dtype=F32) / half)
    ang = pos.astype(F32)[:, None] * inv_freq[None, :]
    cos, sin = jnp.cos(ang), jnp.sin(ang)
    return jnp.concatenate([cos, cos], axis=1), jnp.concatenate([-sin, sin], axis=1)


def kernel(x_prompt, x_sample, cache_sb_k, cache_sb_v, state_ret, page_table, c_prompt, c_sample,
           w_ada, b_ada, g_mix, w_in, b_sb, w_up_sb, w_up_ret, w_o, g_ffn, w_router, b_router,
           w_gate_up, b_gate_up, w_down, b_down, g_final):
    nb, s, d = x_prompt.shape
    nbd, td, _ = x_sample.shape
    depth, n_phys, page, n_sb, dh = cache_sb_k.shape
    _, _, n_ret, dk, dv = state_ret.shape
    n_pages = page_table.shape[1]
    past = n_pages * page
    sbw, rqw, rvw = n_sb * dh, n_ret * dk, n_ret * dv
    n_exp = w_router.shape[-1]
    assert rvw == d and w_in.shape[-1] == 3 * sbw + 2 * rqw + 2 * rvw + 2 * d
    n_tok_p, n_tok_s = nb * s, nbd * td
    n_tok = n_tok_p + n_tok_s

    ret_chunk = _tile(s, 128)
    td_pad = -(-td // V7X_SUBLANES) * V7X_SUBLANES
    cos_p, sin_p = _rotary_tables(jnp.arange(s), dk)
    cos_s, sin_s = _rotary_tables(past + jnp.arange(td_pad), dk)
    log_decay = jnp.log1p(-jnp.exp2(-5.0 - jnp.arange(n_ret, dtype=F32)))

    moe_bm = 512
    n_assign = n_tok * TOP_K
    n_blk = -(-n_assign // moe_bm) + n_exp
    n_slot = n_blk * moe_bm
    route_tile = 512
    t_pad = -(-n_tok // route_tile) * route_tile

    xp = x_prompt
    xs = x_sample.reshape(1, n_tok_s, d)
    c_rows = -(-(nb + nbd) // 16) * 16
    c_all = jnp.concatenate([c_prompt, c_sample, jnp.zeros((c_rows - nb - nbd, d), F32)], axis=0)
    kp, vp, sp, ks, vs, ss = [], [], [], [], [], []

    for l in range(depth):
        mods = jnp.split(_adaln(c_all, w_ada[l], b_ada[l]), N_MOD, axis=1)
        mod_p = [m[:nb].reshape(nb, 1, d) for m in mods]
        mod_s = [jnp.repeat(m[nb:nb + nbd], td, axis=0).reshape(1, n_tok_s, d) for m in mods]

        w = w_in[l]
        o1, o2, o3, o4 = sbw, 2 * sbw, 3 * sbw, 3 * sbw + 2 * rqw
        w_q, w_k, w_v = (w[:, a:b].astype(BF16) for a, b in ((0, o1), (o1, o2), (o2, o3)))
        w_qk_r = w[:, o3:o4].astype(BF16)
        w_rest = w[:, o4:].astype(BF16)
        w_sb_up, w_ret_up, w_out = w_up_sb[l].astype(BF16), w_up_ret[l].astype(BF16), w_o[l].astype(BF16)
        w_rt = w_router[l].T.astype(BF16)

        def in_proj(h):
            return (_matmul(h, w_q, BF16), _matmul(h, w_k, F32), _matmul(h, w_v, F32),
                    _matmul(h, w_qk_r, F32), _matmul(h, w_rest, BF16))

        h = _norm_mod(xp, g_mix[l], mod_p[0], mod_p[1])
        q_sb, k_sb, v_sb, qk_r, rest = in_proj(h)
        o_sb = _sb_prompt(q_sb, k_sb, v_sb, b_sb[l], n_sb)
        o_norm, state_p = _retention(qk_r, rest, cos_p, sin_p, log_decay, None, n_ret, dk, dv, ret_chunk, ret_chunk)
        merged = _merge(o_sb, o_norm, rest, w_sb_up, w_ret_up)
        xp = _matmul_residual(merged, w_out, xp, mod_p[2])
        kp.append(k_sb.reshape(nb, s, n_sb, dh))
        vp.append(v_sb.reshape(nb, s, n_sb, dh))
        sp.append(state_p)
        h2_p, te_p, tw_p = _norm_router(xp, g_ffn[l], mod_p[3], mod_p[4], w_rt, b_router[l])

        h = _norm_mod(xs, g_mix[l], mod_s[0], mod_s[1])
        q_sb, k_sb, v_sb, qk_r, rest = in_proj(h)
        q4 = q_sb.reshape(nbd, td, n_sb, dh)
        own_head = jnp.eye(n_sb, dtype=bool)[None, :, None, :, None]
        q_bd = jnp.where(own_head, q4.transpose(0, 2, 1, 3)[:, :, :, None, :], 0)
        q_bd = jnp.pad(q_bd, ((0, 0), (0, 0), (0, td_pad - td), (0, 0), (0, 0))).reshape(nbd, n_sb * td_pad, sbw)
        k_new = jnp.pad(k_sb.reshape(nbd, td, sbw), ((0, 0), (0, page - td), (0, 0)))
        v_new = jnp.pad(v_sb.reshape(nbd, td, sbw), ((0, 0), (0, page - td), (0, 0)))
        bias_rows = jnp.broadcast_to(jnp.repeat(b_sb[l], td_pad)[:, None], (n_sb * td_pad, V7X_LANES))
        o_sb = _sb_sample(q_bd, k_new, v_new, cache_sb_k[l].reshape(n_phys, page, sbw),
                          cache_sb_v[l].reshape(n_phys, page, sbw), page_table, bias_rows, n_sb, td_pad)
        o_sb = o_sb[:, :td].reshape(1, n_tok_s, sbw).astype(BF16)
        pad_t = ((0, 0), (0, td_pad - td), (0, 0))
        qk_pad = jnp.pad(qk_r.reshape(nbd, td, 2 * rqw), pad_t)
        rest_pad = jnp.pad(rest.reshape(nbd, td, rest.shape[-1]), pad_t)
        o_norm, state_s = _retention(qk_pad, rest_pad, cos_s, sin_s, log_decay, state_ret[l], n_ret, dk, dv,
                                     td_pad, td)
        o_norm = o_norm[:, :td].reshape(1, n_tok_s, rvw)
        merged = _merge(o_sb, o_norm, rest, w_sb_up, w_ret_up)
        xs = _matmul_residual(merged, w_out, xs, mod_s[2])
        ks.append(k_sb.reshape(nbd, td, n_sb, dh))
        vs.append(v_sb.reshape(nbd, td, n_sb, dh))
        ss.append(state_s)
        h2_s, te_s, tw_s = _norm_router(xs, g_ffn[l], mod_s[3], mod_s[4], w_rt, b_router[l])

        top_e = jnp.concatenate([te_p.transpose(1, 0, 2).reshape(TOP_K, n_tok_p), te_s.reshape(TOP_K, n_tok_s),
                                 jnp.full((TOP_K, t_pad - n_tok), -1, I32)], axis=1)
        dest, info = _route(top_e, n_exp, moe_bm, n_blk)
        dest = dest.reshape(TOP_K, t_pad // DEST_TILE, DEST_TILE).transpose(1, 0, 2).reshape(-1, TOP_K * DEST_TILE)
        slots = _dispatch(dest, h2_p.reshape(n_tok_p, d), None, n_slot, 0)
        slots = _dispatch(dest, h2_s.reshape(n_tok_s, d), slots, n_slot, n_tok_p // DEST_TILE)
        yb = _moe_gemm(info, slots, w_gate_up[l].astype(BF16), b_gate_up[l], w_down[l].astype(BF16), b_down[l],
                       moe_bm, n_blk)
        last = l == depth - 1
        xp = _combine(dest, tw_p.transpose(0, 2, 1), xp, mod_p[5], g_final, yb, 0, last)
        xs = _combine(dest, tw_s.transpose(0, 2, 1), xs, mod_s[5], g_final, yb, n_tok_p // DEST_TILE, last)

    return (xp, xs.reshape(nbd, td, d), jnp.stack(kp), jnp.stack(vp), jnp.stack(sp),
            jnp.stack(ks), jnp.stack(vs), jnp.stack(ss))
```

```python
import functools

import jax
import jax.numpy as jnp
from jax import lax
from jax.experimental import pallas as pl
from jax.experimental.pallas import tpu as pltpu

BF16 = jnp.bfloat16
F32 = jnp.float32
I32 = jnp.int32

NORM_EPS = 1e-5
ROPE_BASE = 10000.0
N_MOD = 6
TOP_K = 4
DEST_TILE = 128
SWIGLU_LIMIT = 7.0
SWIGLU_ALPHA = 1.702

V7X_VMEM_BYTES = 64 * 1024 * 1024
V7X_LANES = 128
V7X_SUBLANES = 8
BF16_ROWS = 2 * V7X_SUBLANES
VMEM_CEILING = V7X_VMEM_BYTES * 7 // 8


def _cparams(semantics, vmem_estimate):
    limit = min(max(int(vmem_estimate), 16 * 1024 * 1024), VMEM_CEILING)
    return pltpu.CompilerParams(dimension_semantics=semantics, vmem_limit_bytes=limit)


def _nbytes(shape, dtype):
    n = 1
    for s in shape:
        n *= s
    return n * jnp.dtype(dtype).itemsize


def _tile(n, want):
    t = min(n, want)
    while n % t:
        t -= 1
    return t


def _dot(a, b):
    return jnp.dot(a, b, preferred_element_type=F32)


def _dot_nt(a, b):
    return lax.dot_general(a, b, (((1,), (1,)), ((), ())), preferred_element_type=F32)


def _dot_tn(a, b):
    return lax.dot_general(a, b, (((0,), (0,)), ((), ())), preferred_element_type=F32)


def _sigmoid(x):
    return 1.0 / (1.0 + jnp.exp(-x))


LOG2_E = 1.4426950408889634


def _tri_ones(n):
    row = lax.broadcasted_iota(I32, (n, n + V7X_LANES), 0)
    col = lax.broadcasted_iota(I32, (n, n + V7X_LANES), 1)
    return jnp.where(jnp.logical_or(col >= n, row > col), 1.0, 0.0).astype(BF16)


def _ada_kernel(c_ref, w_ref, b_ref, o_ref):
    c = c_ref[...]
    s = (c * _sigmoid(c)).astype(BF16)
    o_ref[...] = _dot(s, w_ref[...].astype(BF16)) + b_ref[...]


def _adaln(c, w, b):
    m, d = c.shape
    n = w.shape[1]
    tn = _tile(n, 1024)
    est = 2 * (_nbytes((d, tn), F32) + _nbytes((m, tn), F32)) + _nbytes((d, tn), BF16) + _nbytes((m, d), F32) * 4
    return pl.pallas_call(
        _ada_kernel,
        grid=(n // tn,),
        in_specs=[pl.BlockSpec((m, d), lambda j: (0, 0)),
                  pl.BlockSpec((d, tn), lambda j: (0, j)),
                  pl.BlockSpec((1, tn), lambda j: (0, j))],
        out_specs=pl.BlockSpec((m, tn), lambda j: (0, j)),
        out_shape=jax.ShapeDtypeStruct((m, n), F32),
        compiler_params=_cparams(("parallel",), est + (8 << 20)),
        name="adaln",
    )(c, w, b.reshape(1, n))


def _normed(x_ref, g_ref, sh_ref, sc_ref):
    x = x_ref[0]
    y = x * lax.rsqrt(jnp.mean(x * x, axis=-1, keepdims=True) + NORM_EPS) * g_ref[...]
    return y * (1.0 + sc_ref[0]) + sh_ref[0]


def _norm_mod_kernel(x_ref, g_ref, sh_ref, sc_ref, h_ref):
    h_ref[0] = _normed(x_ref, g_ref, sh_ref, sc_ref).astype(h_ref.dtype)


def _norm_router_kernel(x_ref, g_ref, sh_ref, sc_ref, wr_ref, br_ref, h_ref, e_ref, p_ref):
    h = _normed(x_ref, g_ref, sh_ref, sc_ref)
    h_ref[0] = h
    logits = _dot_nt(wr_ref[...], h.astype(BF16)) + br_ref[...]
    n_exp = logits.shape[0]
    iota_e = lax.broadcasted_iota(I32, logits.shape, 0)
    tops = []
    for k in range(TOP_K):
        m = jnp.max(logits, axis=0, keepdims=True)
        idx = jnp.min(jnp.where(logits == m, iota_e, n_exp), axis=0, keepdims=True)
        e_ref[0, k:k + 1, :] = idx
        tops.append(m)
        logits = jnp.where(iota_e == idx, -jnp.inf, logits)
    ex = [jnp.exp(t - tops[0]) for t in tops]
    inv = 1.0 / (ex[0] + ex[1] + ex[2] + ex[3])
    for k in range(TOP_K):
        p_ref[0, k:k + 1, :] = ex[k] * inv


def _mod_spec(mod, ts, d):
    if mod.shape[1] == 1:
        return pl.BlockSpec((1, 1, d), lambda b, i: (b, 0, 0))
    return pl.BlockSpec((1, ts, d), lambda b, i: (b, i, 0))


def _norm_mod(x, g, sh, sc):
    nb, s, d = x.shape
    ts = _tile(s, 512)
    est = 2 * (_nbytes((ts, d), F32) + _nbytes((ts, d), BF16)) + 4 * _nbytes((ts, d), F32)
    return pl.pallas_call(
        _norm_mod_kernel,
        grid=(nb, s // ts),
        in_specs=[pl.BlockSpec((1, ts, d), lambda b, i: (b, i, 0)),
                  pl.BlockSpec((1, d), lambda b, i: (0, 0)),
                  _mod_spec(sh, ts, d), _mod_spec(sc, ts, d)],
        out_specs=pl.BlockSpec((1, ts, d), lambda b, i: (b, i, 0)),
        out_shape=jax.ShapeDtypeStruct((nb, s, d), BF16),
        compiler_params=_cparams(("parallel", "parallel"), est + (8 << 20)),
        name="norm_mod",
    )(x, g.reshape(1, d), sh, sc)


def _norm_router(x, g, sh, sc, w_router_t, b_router):
    nb, s, d = x.shape
    n_exp = w_router_t.shape[0]
    ts = _tile(s, 512)
    est = 4 * _nbytes((ts, d), F32) + 6 * _nbytes((ts, d), F32)
    return pl.pallas_call(
        _norm_router_kernel,
        grid=(nb, s // ts),
        in_specs=[pl.BlockSpec((1, ts, d), lambda b, i: (b, i, 0)),
                  pl.BlockSpec((1, d), lambda b, i: (0, 0)),
                  _mod_spec(sh, ts, d), _mod_spec(sc, ts, d),
                  pl.BlockSpec((n_exp, d), lambda b, i: (0, 0)),
                  pl.BlockSpec((n_exp, 1), lambda b, i: (0, 0))],
        out_specs=[pl.BlockSpec((1, ts, d), lambda b, i: (b, i, 0)),
                   pl.BlockSpec((1, TOP_K, ts), lambda b, i: (b, 0, i)),
                   pl.BlockSpec((1, TOP_K, ts), lambda b, i: (b, 0, i))],
        out_shape=[jax.ShapeDtypeStruct((nb, s, d), F32),
                   jax.ShapeDtypeStruct((nb, TOP_K, s), I32),
                   jax.ShapeDtypeStruct((nb, TOP_K, s), F32)],
        compiler_params=_cparams(("parallel", "parallel"), est + (8 << 20)),
        name="norm_router",
    )(x, g.reshape(1, d), sh, sc, w_router_t, b_router.reshape(n_exp, 1))


def _mm_kernel(a_ref, w_ref, *o_refs):
    acc = _dot(a_ref[0], w_ref[...])
    for o_ref in o_refs:
        o_ref[0] = acc.astype(o_ref.dtype)


def _mm_res_kernel(a_ref, w_ref, res_ref, gate_ref, o_ref):
    o_ref[0] = res_ref[0] + gate_ref[0] * _dot(a_ref[0], w_ref[...])


def _mm_tiles(s, k, n):
    return _tile(s, 1024), _tile(n, 1024)


def _matmul(a, w, *out_dtypes):
    nb, s, k = a.shape
    n = w.shape[1]
    tm, tn = _mm_tiles(s, k, n)
    est = 2 * (_nbytes((tm, k), BF16) + _nbytes((k, tn), BF16) + 2 * _nbytes((tm, tn), F32)) + _nbytes((tm, tn), F32)
    outs = pl.pallas_call(
        _mm_kernel,
        grid=(nb, s // tm, n // tn),
        in_specs=[pl.BlockSpec((1, tm, k), lambda b, i, j: (b, i, 0)),
                  pl.BlockSpec((k, tn), lambda b, i, j: (0, j))],
        out_specs=[pl.BlockSpec((1, tm, tn), lambda b, i, j: (b, i, j)) for _ in out_dtypes],
        out_shape=[jax.ShapeDtypeStruct((nb, s, n), dt) for dt in out_dtypes],
        compiler_params=_cparams(("parallel", "parallel", "parallel"), est + (8 << 20)),
        name="matmul",
    )(a, w)
    return outs[0] if len(outs) == 1 else outs


def _matmul_residual(a, w, res, gate):
    nb, s, k = a.shape
    n = w.shape[1]
    tm, tn = _mm_tiles(s, k, n)
    if gate.shape[1] == 1:
        gate_spec = pl.BlockSpec((1, 1, tn), lambda b, i, j: (b, 0, j))
    else:
        gate_spec = pl.BlockSpec((1, tm, tn), lambda b, i, j: (b, i, j))
    est = 2 * (_nbytes((tm, k), BF16) + _nbytes((k, tn), BF16) + 3 * _nbytes((tm, tn), F32)) + _nbytes((tm, tn), F32)
    return pl.pallas_call(
        _mm_res_kernel,
        grid=(nb, s // tm, n // tn),
        in_specs=[pl.BlockSpec((1, tm, k), lambda b, i, j: (b, i, 0)),
                  pl.BlockSpec((k, tn), lambda b, i, j: (0, j)),
                  pl.BlockSpec((1, tm, tn), lambda b, i, j: (b, i, j)),
                  gate_spec],
        out_specs=pl.BlockSpec((1, tm, tn), lambda b, i, j: (b, i, j)),
        out_shape=jax.ShapeDtypeStruct((nb, s, n), F32),
        compiler_params=_cparams(("parallel", "parallel", "parallel"), est + (8 << 20)),
        name="matmul_residual",
    )(a, w, res, gate)


def _sb_block(q, k, v, bias, scale, carry, tri_ones, mask):
    a, carry = _sb_weights(_dot_nt(q, k) * scale + bias, carry, tri_ones, mask)
    return _dot(a, v), carry


def _sb_weights(z2, carry, tri_ones, mask):
    bk = z2.shape[1]
    soft = jnp.log2(1.0 + jnp.exp2(-jnp.abs(z2)))
    lb = jnp.minimum(z2, 0.0) - soft
    lk = lb - z2
    if mask is not None:
        lk = jnp.where(mask, lk, 0.0)
    cs = _dot(lk.astype(BF16), tri_ones)
    between = cs[:, :bk] + jnp.concatenate([carry] * (bk // V7X_LANES), axis=1)
    a = jnp.exp2(lb + between)
    if mask is not None:
        a = jnp.where(mask, a, 0.0)
    return a.astype(BF16), carry + cs[:, bk:]


def _sbp_kernel(bias_ref, q_ref, k_ref, v_ref, o_ref, acc_ref, carry_ref, *, blk, scale, group, dh):
    qi = pl.program_id(2)
    biases = [bias_ref[pl.program_id(1) * group + g] * LOG2_E for g in range(group)]
    tri_ones = _tri_ones(blk)
    row = lax.broadcasted_iota(I32, (blk, blk), 0)
    col = lax.broadcasted_iota(I32, (blk, blk), 1)

    def sweep(start, carries, mask):
        outs, new = [], []
        for g in range(group):
            cols = slice(g * dh, (g + 1) * dh)
            out, carry = _sb_block(q_ref[0, :, cols], k_ref[0, pl.ds(start, blk), cols],
                                   v_ref[0, pl.ds(start, blk), cols], biases[g], scale, carries[g], tri_ones, mask)
            outs.append(out)
            new.append(carry)
        return jnp.concatenate(outs, axis=1), jnp.concatenate(new, axis=1)

    zero = jnp.zeros((blk, V7X_LANES), F32)
    out, carry = sweep(pl.multiple_of(qi * blk, blk), [zero] * group, col < row)
    acc_ref[...] = out
    carry_ref[...] = carry

    def body(j, _):
        carry = carry_ref[...]
        out, carry = sweep(pl.multiple_of((qi - 1 - j) * blk, blk),
                           [carry[:, g * V7X_LANES:(g + 1) * V7X_LANES] for g in range(group)], None)
        acc_ref[...] += out
        carry_ref[...] = carry
        return 0

    lax.fori_loop(0, qi, body, 0)
    o_ref[0] = acc_ref[...].astype(o_ref.dtype)


def _sb_prompt(q, k, v, bias, n_heads):
    nb, s, w = q.shape
    dh = w // n_heads
    blk = _tile(s, 256)
    group = _tile(n_heads, 8)
    gw = group * dh
    est = 8 * _nbytes((s, gw), BF16) + 4 * _nbytes((blk, gw), F32) + 12 * group * _nbytes((blk, blk), F32)
    return pl.pallas_call(
        functools.partial(_sbp_kernel, blk=blk, scale=dh ** -0.5 * LOG2_E, group=group, dh=dh),
        grid=(nb, n_heads // group, s // blk),
        in_specs=[pl.BlockSpec(memory_space=pltpu.SMEM),
                  pl.BlockSpec((1, blk, gw), lambda b, h, i: (b, i, h)),
                  pl.BlockSpec((1, s, gw), lambda b, h, i: (b, 0, h)),
                  pl.BlockSpec((1, s, gw), lambda b, h, i: (b, 0, h))],
        out_specs=pl.BlockSpec((1, blk, gw), lambda b, h, i: (b, i, h)),
        scratch_shapes=[pltpu.VMEM((blk, gw), F32), pltpu.VMEM((blk, group * V7X_LANES), F32)],
        out_shape=jax.ShapeDtypeStruct((nb, s, w), BF16),
        compiler_params=_cparams(("parallel", "parallel", "arbitrary"), est + (8 << 20)),
        name="sb_prompt",
    )(bias, q, k, v)


def _sbs_kernel(pt_ref, bias_ref, q_ref, kn_ref, vn_ref, *refs, pages_per_step, rows_per_head, n_heads, page, scale):
    del pt_ref
    k_refs = refs[:pages_per_step]
    v_refs = refs[pages_per_step:2 * pages_per_step]
    o_ref, acc_ref, carry_ref = refs[2 * pages_per_step:]
    s = pl.program_id(1)
    rph = rows_per_head
    qs = [q_ref[0, h * rph:(h + 1) * rph, :] for h in range(n_heads)]
    bias = bias_ref[...] * LOG2_E
    tri_ones = _tri_ones(page)

    def head_rows(ref, h):
        return ref[0, pl.ds(h, page, stride=n_heads), :].astype(BF16)

    def sweep(pages, mask):
        carry = carry_ref[...]
        total = None
        for k_ref, v_ref in pages:
            z = jnp.concatenate([_dot_nt(qs[h], head_rows(k_ref, h)) for h in range(n_heads)], axis=0)
            a, carry = _sb_weights(z * scale + bias, carry, tri_ones, mask)
            out = jnp.concatenate([_dot(a[h * rph:(h + 1) * rph], head_rows(v_ref, h)) for h in range(n_heads)],
                                  axis=0)
            total = out if total is None else total + out
        acc_ref[...] += total
        carry_ref[...] = carry

    @pl.when(s == 0)
    def _():
        acc_ref[...] = jnp.zeros_like(acc_ref)
        carry_ref[...] = jnp.zeros_like(carry_ref)
        row = lax.broadcasted_iota(I32, (n_heads * rph, page), 0)
        col = lax.broadcasted_iota(I32, (n_heads * rph, page), 1)
        sweep([(kn_ref, vn_ref)], col < row % rph)

    sweep(list(zip(k_refs, v_refs)), None)

    @pl.when(s == pl.num_programs(1) - 1)
    def _():
        o_ref[0] = acc_ref[...]


def _sb_sample(q, k_new, v_new, cache_k, cache_v, page_table, bias_rows, n_heads, rows_per_head):
    nbd, r, dh = q.shape
    page = cache_k.shape[1] // n_heads
    n_pages = page_table.shape[1]
    pps = _tile(n_pages, 4)
    n_steps = n_pages // pps
    blk = (1, page * n_heads, dh)

    def page_spec(p):
        return pl.BlockSpec(blk, lambda b, s, pt: (pt[b, n_pages - 1 - (s * pps + p)], 0, 0))

    in_specs = [pl.BlockSpec((r, V7X_LANES), lambda b, s, pt: (0, 0)),
                pl.BlockSpec((1, r, dh), lambda b, s, pt: (b, 0, 0)),
                pl.BlockSpec(blk, lambda b, s, pt: (b, 0, 0)),
                pl.BlockSpec(blk, lambda b, s, pt: (b, 0, 0))]
    in_specs += [page_spec(p) for p in range(pps)] * 2
    est = 2 * (2 * pps + 2) * _nbytes(blk, F32) + 16 * _nbytes((r, page), F32) + 4 * _nbytes(blk, BF16)
    return pl.pallas_call(
        functools.partial(_sbs_kernel, pages_per_step=pps, rows_per_head=rows_per_head, n_heads=n_heads,
                          page=page, scale=dh ** -0.5 * LOG2_E),
        grid_spec=pltpu.PrefetchScalarGridSpec(
            num_scalar_prefetch=1,
            grid=(nbd, n_steps),
            in_specs=in_specs,
            out_specs=pl.BlockSpec((1, r, dh), lambda b, s, pt: (b, 0, 0)),
            scratch_shapes=[pltpu.VMEM((r, dh), F32), pltpu.VMEM((r, V7X_LANES), F32)]),
        out_shape=jax.ShapeDtypeStruct((nbd, r, dh), F32),
        compiler_params=_cparams(("parallel", "arbitrary"), est + (8 << 20)),
        name="sb_sample",
    )(page_table, bias_rows, q, k_new, v_new, *([cache_k] * pps), *([cache_v] * pps))


def _ret_kernel(lg_ref, q_ref, k_ref, v_ref, cos_ref, sin_ref, *refs, c_true, has_init, kscale, n_heads, dv):
    if has_init:
        s0_ref, o_ref, sout_ref, st_ref = refs
    else:
        o_ref, sout_ref, st_ref = refs
    c = pl.program_id(1)

    @pl.when(c == 0)
    def _():
        if has_init:
            st_ref[...] = s0_ref[0]
        else:
            st_ref[...] = jnp.zeros_like(st_ref)

    cos = cos_ref[...]
    sin = sin_ref[...]
    n_rows, dk = cos.shape

    def rot(x):
        return x * cos + pltpu.roll(x, dk // 2, 1) * sin

    n = lax.broadcasted_iota(I32, (n_rows, n_rows), 0)
    m = lax.broadcasted_iota(I32, (n_rows, n_rows), 1)
    diff = (n - m).astype(F32)
    pos = lax.broadcasted_iota(I32, (n_rows, dk), 0).astype(F32)
    outs = []
    for h in range(n_heads):
        lg = lg_ref[h]
        q = rot(q_ref[0, :, h * dk:(h + 1) * dk])
        k = rot(k_ref[0, :, h * dk:(h + 1) * dk]) * kscale
        v = v_ref[0, :, h * dv:(h + 1) * dv]
        decay = jnp.where(diff >= 0.0, jnp.exp(jnp.maximum(diff, 0.0) * lg), 0.0)
        scores = _dot_nt(q.astype(BF16), k.astype(BF16)) * decay
        q_dec = q * jnp.exp((pos + 1.0) * lg)
        k_dec = k * jnp.exp((c_true - 1.0 - pos) * lg)
        st = st_ref[h]
        o = _dot(scores.astype(BF16), v) + _dot(q_dec.astype(BF16), st.astype(BF16))
        st_ref[h] = jnp.exp(jnp.full((1, dv), c_true * lg, F32)) * st + _dot_tn(k_dec.astype(BF16), v)
        mu = jnp.mean(o, axis=-1, keepdims=True)
        dev = o - mu
        var = jnp.mean(dev * dev, axis=-1, keepdims=True)
        outs.append((dev * lax.rsqrt(var + NORM_EPS)).astype(o_ref.dtype))
    o_ref[0] = jnp.concatenate(outs, axis=1)

    @pl.when(c == pl.num_programs(1) - 1)
    def _():
        sout_ref[0] = st_ref[...]


def _retention(qk, v_src, cos2, sin2, log_decay, state0, n_heads, dk, dv, chunk, c_true):
    nb, s, _ = qk.shape
    nc = s // chunk
    qw, vw = n_heads * dk, n_heads * dv
    in_specs = [pl.BlockSpec(memory_space=pltpu.SMEM),
                pl.BlockSpec((1, chunk, qw), lambda b, c: (b, c, 0)),
                pl.BlockSpec((1, chunk, qw), lambda b, c: (b, c, 1)),
                pl.BlockSpec((1, chunk, vw), lambda b, c: (b, c, 0)),
                pl.BlockSpec((chunk, dk), lambda b, c: (c, 0)),
                pl.BlockSpec((chunk, dk), lambda b, c: (c, 0))]
    args = [log_decay, qk, qk, v_src, cos2, sin2]
    if state0 is not None:
        in_specs.append(pl.BlockSpec((1, n_heads, dk, dv), lambda b, c: (b, 0, 0, 0)))
        args.append(state0)
    est = (7 * _nbytes((n_heads, dk, dv), F32) + 4 * _nbytes((chunk, qw), F32) + 4 * _nbytes((chunk, vw), BF16)
           + n_heads * (8 * _nbytes((chunk, dv), F32) + 6 * _nbytes((chunk, chunk), F32)))
    return pl.pallas_call(
        functools.partial(_ret_kernel, c_true=float(c_true), has_init=state0 is not None, kscale=dk ** -0.5,
                          n_heads=n_heads, dv=dv),
        grid=(nb, nc),
        in_specs=in_specs,
        out_specs=[pl.BlockSpec((1, chunk, vw), lambda b, c: (b, c, 0)),
                   pl.BlockSpec((1, n_heads, dk, dv), lambda b, c: (b, 0, 0, 0))],
        out_shape=[jax.ShapeDtypeStruct((nb, s, vw), BF16),
                   jax.ShapeDtypeStruct((nb, n_heads, dk, dv), F32)],
        scratch_shapes=[pltpu.VMEM((n_heads, dk, dv), F32)],
        compiler_params=_cparams(("parallel", "arbitrary"), est + (8 << 20)),
        name="retention",
    )(*args)


def _merge_kernel(osb_ref, on_ref, g_ref, asb_ref, ar_ref, wsb_ref, wret_ref, o_ref):
    y_sb = _dot(osb_ref[0], wsb_ref[...])
    g = g_ref[0].astype(F32)
    u = (g * _sigmoid(g)) * on_ref[0].astype(F32)
    y_r = _dot(u.astype(BF16), wret_ref[...])
    merged = _sigmoid(asb_ref[0].astype(F32)) * y_sb + _sigmoid(ar_ref[0].astype(F32)) * y_r
    o_ref[0] = merged.astype(o_ref.dtype)


def _merge(o_sb, o_norm, rest, w_up_sb, w_up_ret):
    nb, s, sbw = o_sb.shape
    rvw = o_norm.shape[2]
    d = w_up_sb.shape[1]
    tm = _tile(s, 256)
    est = (2 * (_nbytes((sbw, d), BF16) + _nbytes((rvw, d), BF16))
           + 2 * (_nbytes((tm, sbw), BF16) + 5 * _nbytes((tm, d), BF16)) + 8 * _nbytes((tm, d), F32))
    return pl.pallas_call(
        _merge_kernel,
        grid=(nb, s // tm),
        in_specs=[pl.BlockSpec((1, tm, sbw), lambda b, i: (b, i, 0)),
                  pl.BlockSpec((1, tm, rvw), lambda b, i: (b, i, 0)),
                  pl.BlockSpec((1, tm, rvw), lambda b, i: (b, i, 1)),
                  pl.BlockSpec((1, tm, d), lambda b, i: (b, i, 2)),
                  pl.BlockSpec((1, tm, d), lambda b, i: (b, i, 3)),
                  pl.BlockSpec((sbw, d), lambda b, i: (0, 0)),
                  pl.BlockSpec((rvw, d), lambda b, i: (0, 0))],
        out_specs=pl.BlockSpec((1, tm, d), lambda b, i: (b, i, 0)),
        out_shape=jax.ShapeDtypeStruct((nb, s, d), BF16),
        compiler_params=_cparams(("parallel", "parallel"), est + (8 << 20)),
        name="merge",
    )(o_sb, o_norm, rest, rest, rest, w_up_sb, w_up_ret)


def _route_kernel(te_ref, dest_ref, info_ref, cnt_ref, base_ref, *, n_exp, bm, n_info):
    phase = pl.program_id(0)
    i = pl.program_id(1)
    te = te_ref[...]
    tm = te.shape[1]
    iota_e = lax.broadcasted_iota(I32, (n_exp, tm), 0)
    ones = jnp.ones((tm, V7X_LANES), BF16)
    onehots = [jnp.where(iota_e == te[k:k + 1, :], 1.0, 0.0).astype(BF16) for k in range(TOP_K)]

    @pl.when(jnp.logical_and(phase == 0, i == 0))
    def _():
        cnt_ref[...] = jnp.zeros_like(cnt_ref)

    @pl.when(phase == 0)
    def _():
        cnt_ref[...] += _dot(onehots[0] + onehots[1] + onehots[2] + onehots[3], ones)

    @pl.when(jnp.logical_and(phase == 1, i == 0))
    def _():
        counts = cnt_ref[...]
        padded = jnp.floor((counts + (bm - 1.0)) * (1.0 / bm)) * bm
        er = lax.broadcasted_iota(I32, (n_exp, n_exp), 0)
        ec = lax.broadcasted_iota(I32, (n_exp, n_exp), 1)
        lower = jnp.where(ec < er, 1.0, 0.0)
        pstart = jnp.dot(lower, padded, preferred_element_type=F32, precision=lax.Precision.HIGHEST)
        base_ref[...] = pstart
        reps = n_info // V7X_LANES
        pend = jnp.concatenate([pstart + padded] * reps, axis=1)
        used = jnp.concatenate([pstart + counts] * reps, axis=1)
        blk_start = (lax.broadcasted_iota(I32, (n_exp, n_info), 1) * bm).astype(F32)
        blk_e = jnp.minimum(jnp.sum(jnp.where(pend <= blk_start, 1.0, 0.0), axis=0, keepdims=True), n_exp - 1.0)
        mine = lax.broadcasted_iota(I32, (n_exp, n_info), 0).astype(F32) == blk_e
        valid = jnp.sum(jnp.where(mine, used - blk_start, 0.0), axis=0, keepdims=True)
        valid = jnp.clip(valid, 0.0, float(bm))
        n_active = jnp.sum(padded[:, :1], axis=0, keepdims=True) * (1.0 / bm)
        info_ref[...] = jnp.zeros_like(info_ref)
        info_ref[0:1, :] = blk_e.astype(I32)
        info_ref[1:2, :] = valid.astype(I32)
        info_ref[2:3, :] = jnp.broadcast_to(n_active, (1, n_info)).astype(I32)

    @pl.when(phase == 1)
    def _():
        tri = jnp.where(lax.broadcasted_iota(I32, (tm, tm), 0) < lax.broadcasted_iota(I32, (tm, tm), 1),
                        1.0, 0.0).astype(BF16)
        base = base_ref[...]
        for k in range(TOP_K):
            before = _dot(onehots[k], tri) + jnp.concatenate([base] * (tm // V7X_LANES), axis=1)
            dest = jnp.sum(jnp.where(onehots[k] > 0, before, 0.0), axis=0, keepdims=True)
            dest_ref[k:k + 1, :] = dest.astype(I32)
            base = base + _dot(onehots[k], ones)
        base_ref[...] = base


def _route(top_e, n_exp, bm, n_blk):
    _, tpad = top_e.shape
    tm = _tile(tpad, 512)
    n_info = -(-n_blk // V7X_LANES) * V7X_LANES
    est = 8 * _nbytes((tm, tm), F32) + 16 * _nbytes((n_exp, tm), F32)
    return pl.pallas_call(
        functools.partial(_route_kernel, n_exp=n_exp, bm=bm, n_info=n_info),
        grid=(2, tpad // tm),
        in_specs=[pl.BlockSpec((TOP_K, tm), lambda p, i: (0, i))],
        out_specs=[pl.BlockSpec((TOP_K, tm), lambda p, i: (0, i * p)),
                   pl.BlockSpec((V7X_SUBLANES, n_info), lambda p, i: (0, 0))],
        out_shape=[jax.ShapeDtypeStruct((TOP_K, tpad), I32),
                   jax.ShapeDtypeStruct((V7X_SUBLANES, n_info), I32)],
        scratch_shapes=[pltpu.VMEM((n_exp, V7X_LANES), F32), pltpu.VMEM((n_exp, V7X_LANES), F32)],
        compiler_params=_cparams(("arbitrary", "arbitrary"), est + (8 << 20)),
        name="moe_route",
    )(top_e)


def _row_copy(src_ref, src_row, dst_ref, dst_row, sem):
    return pltpu.make_async_copy(src_ref.at[pl.ds(src_row, 1)], dst_ref.at[pl.ds(dst_row, 1)], sem)


def _dispatch_kernel(dest_ref, h_ref, *refs, rows, first_tile):
    xs_ref, sem = refs[-2:]
    tile = first_tile + pl.program_id(0)

    def issue(r, _):
        for k in range(TOP_K):
            _row_copy(h_ref, r, xs_ref, dest_ref[tile, k * rows + r], sem).start()
        return 0

    lax.fori_loop(0, rows, issue, 0)

    def drain(r, _):
        for k in range(TOP_K):
            _row_copy(h_ref, 0, xs_ref, 0, sem).wait()
        return 0

    lax.fori_loop(0, rows, drain, 0)


def _dispatch(dest, h, xs, n_slot, first_tile):
    t, d = h.shape
    rows = DEST_TILE
    assert t % rows == 0
    in_specs = [pl.BlockSpec((rows, d), lambda i, dest: (i, 0))]
    args = [dest, h]
    aliases = {}
    if xs is not None:
        in_specs.append(pl.BlockSpec(memory_space=pl.ANY))
        args.append(xs)
        aliases = {2: 0}
    return pl.pallas_call(
        functools.partial(_dispatch_kernel, rows=rows, first_tile=first_tile),
        grid_spec=pltpu.PrefetchScalarGridSpec(
            num_scalar_prefetch=1,
            grid=(t // rows,),
            in_specs=in_specs,
            out_specs=pl.BlockSpec(memory_space=pl.ANY),
            scratch_shapes=[pltpu.SemaphoreType.DMA]),
        out_shape=jax.ShapeDtypeStruct((n_slot, d), F32),
        input_output_aliases=aliases,
        compiler_params=_cparams(("arbitrary",), 4 * _nbytes((rows, d), F32) + (8 << 20)),
        name="moe_dispatch",
    )(*args)


def _moe_gemm_kernel(info_ref, x_ref, wg_ref, wu_ref, bg_ref, bu_ref, wd_ref, bd_ref, o_ref, xb_ref):
    i = pl.program_id(0)
    j = pl.program_id(1)

    @pl.when(i < info_ref[2, 0])
    def _():
        @pl.when(j == 0)
        def _():
            row = lax.broadcasted_iota(I32, x_ref.shape, 0)
            xb_ref[...] = jnp.where(row < info_ref[1, i], x_ref[...], 0.0).astype(BF16)

        xb = xb_ref[...]
        gate = jnp.minimum(_dot(xb, wg_ref[0].astype(BF16)) + bg_ref[0], SWIGLU_LIMIT)
        up = jnp.clip(_dot(xb, wu_ref[0].astype(BF16)) + bu_ref[0], -SWIGLU_LIMIT, SWIGLU_LIMIT)
        act = (up + 1.0) * gate * _sigmoid(SWIGLU_ALPHA * gate)
        part = _dot(act.astype(BF16), wd_ref[0].astype(BF16))

        @pl.when(j == 0)
        def _():
            o_ref[...] = part + bd_ref[0]

        @pl.when(j > 0)
        def _():
            o_ref[...] += part


def _moe_gemm(info, xs, w_gate_up, b_gate_up, w_down, b_down, bm, n_blk):
    n_slot, d = xs.shape
    n_exp, _, f2 = w_gate_up.shape
    f = f2 // 2
    tf = _tile(f, 512)
    nf = f // tf

    def blk(i, info):
        return jnp.minimum(i, info[2, 0] - 1)

    def ff(i, j, info):
        return jnp.where(i < info[2, 0], j, nf - 1)

    in_specs = [pl.BlockSpec((bm, d), lambda i, j, info: (blk(i, info), 0)),
                pl.BlockSpec((1, d, tf), lambda i, j, info: (info[0, blk(i, info)], 0, ff(i, j, info))),
                pl.BlockSpec((1, d, tf), lambda i, j, info: (info[0, blk(i, info)], 0, nf + ff(i, j, info))),
                pl.BlockSpec((1, 1, tf), lambda i, j, info: (info[0, blk(i, info)], 0, ff(i, j, info))),
                pl.BlockSpec((1, 1, tf), lambda i, j, info: (info[0, blk(i, info)], 0, nf + ff(i, j, info))),
                pl.BlockSpec((1, tf, d), lambda i, j, info: (info[0, blk(i, info)], ff(i, j, info), 0)),
                pl.BlockSpec((1, 1, d), lambda i, j, info: (info[0, blk(i, info)], 0, 0))]
    est = (2 * (2 * _nbytes((bm, d), F32) + 3 * _nbytes((d, tf), F32)) + _nbytes((bm, d), BF16)
           + 3 * _nbytes((d, tf), BF16) + 6 * _nbytes((bm, tf), F32))
    return pl.pallas_call(
        _moe_gemm_kernel,
        grid_spec=pltpu.PrefetchScalarGridSpec(
            num_scalar_prefetch=1,
            grid=(n_blk, nf),
            in_specs=in_specs,
            out_specs=pl.BlockSpec((bm, d), lambda i, j, info: (blk(i, info), 0)),
            scratch_shapes=[pltpu.VMEM((bm, d), BF16)]),
        out_shape=jax.ShapeDtypeStruct((n_slot, d), F32),
        compiler_params=_cparams(("arbitrary", "arbitrary"), est + (8 << 20)),
        name="moe_gemm",
    )(info, xs, w_gate_up, w_gate_up, b_gate_up.reshape(n_exp, 1, f2), b_gate_up.reshape(n_exp, 1, f2),
      w_down, b_down.reshape(n_exp, 1, d))


def _combine_kernel(dest_ref, p_ref, res_ref, gate_ref, g_ref, yb_ref, o_ref, buf_ref, sem, *, rows, first_tile,
                    final_norm):
    tile = first_tile + pl.program_id(0) * pl.num_programs(1) + pl.program_id(1)

    def issue(r, _):
        for k in range(TOP_K):
            _row_copy(yb_ref, dest_ref[tile, k * rows + r], buf_ref.at[k], r, sem).start()
        return 0

    lax.fori_loop(0, rows, issue, 0)

    def drain(r, _):
        for k in range(TOP_K):
            _row_copy(yb_ref, 0, buf_ref.at[k], 0, sem).wait()
        return 0

    lax.fori_loop(0, rows, drain, 0)
    p = p_ref[0]
    y = p[:, 0:1] * buf_ref[0]
    for k in range(1, TOP_K):
        y = y + p[:, k:k + 1] * buf_ref[k]
    x = res_ref[0] + gate_ref[0] * y
    if final_norm:
        x = x * lax.rsqrt(jnp.mean(x * x, axis=-1, keepdims=True) + NORM_EPS) * g_ref[...]
    o_ref[0] = x


def _combine(dest, probs, res, gate, g_final, yb, first_tile, final_norm):
    nb, s, d = res.shape
    rows = DEST_TILE
    assert s % rows == 0
    if gate.shape[1] == 1:
        gate_spec = pl.BlockSpec((1, 1, d), lambda b, i, dest: (b, 0, 0))
    else:
        gate_spec = pl.BlockSpec((1, rows, d), lambda b, i, dest: (b, i, 0))
    est = _nbytes((TOP_K, rows, d), F32) + 6 * _nbytes((rows, d), F32) + 4 * _nbytes((rows, d), F32)
    return pl.pallas_call(
        functools.partial(_combine_kernel, rows=rows, first_tile=first_tile, final_norm=final_norm),
        grid_spec=pltpu.PrefetchScalarGridSpec(
            num_scalar_prefetch=1,
            grid=(nb, s // rows),
            in_specs=[pl.BlockSpec((1, rows, TOP_K), lambda b, i, dest: (b, i, 0)),
                      pl.BlockSpec((1, rows, d), lambda b, i, dest: (b, i, 0)),
                      gate_spec,
                      pl.BlockSpec((1, d), lambda b, i, dest: (0, 0)),
                      pl.BlockSpec(memory_space=pl.ANY)],
            out_specs=pl.BlockSpec((1, rows, d), lambda b, i, dest: (b, i, 0)),
            scratch_shapes=[pltpu.VMEM((TOP_K, rows, d), F32), pltpu.SemaphoreType.DMA]),
        out_shape=jax.ShapeDtypeStruct((nb, s, d), F32),
        compiler_params=_cparams(("arbitrary", "arbitrary"), est + (8 << 20)),
        name="moe_combine",
    )(dest, probs, res, gate, g_final.reshape(1, d), yb)


def _rotary_tables(pos, dk):
    half = dk // 2
    inv_freq = ROPE_BASE ** (-jnp.arange(half, dtype=F32) / half)
    ang = pos.astype(F32)[:, None] * inv_freq[None, :]
    cos, sin = jnp.cos(ang), jnp.sin(ang)
    return jnp.concatenate([cos, cos], axis=1), jnp.concatenate([-sin, sin], axis=1)


def kernel(x_prompt, x_sample, cache_sb_k, cache_sb_v, state_ret, page_table, c_prompt, c_sample,
           w_ada, b_ada, g_mix, w_in, b_sb, w_up_sb, w_up_ret, w_o, g_ffn, w_router, b_router,
           w_gate_up, b_gate_up, w_down, b_down, g_final):
    nb, s, d = x_prompt.shape
    nbd, td, _ = x_sample.shape
    depth, n_phys, page, n_sb, dh = cache_sb_k.shape
    _, _, n_ret, dk, dv = state_ret.shape
    n_pages = page_table.shape[1]
    past = n_pages * page
    sbw, rqw, rvw = n_sb * dh, n_ret * dk, n_ret * dv
    n_exp = w_router.shape[-1]
    assert rvw == d and w_in.shape[-1] == 3 * sbw + 2 * rqw + 2 * rvw + 2 * d
    n_tok_p, n_tok_s = nb * s, nbd * td
    n_tok = n_tok_p + n_tok_s

    ret_chunk = _tile(s, 128)
    td_pad = -(-td // V7X_SUBLANES) * V7X_SUBLANES
    tq_pad = -(-td // BF16_ROWS) * BF16_ROWS
    cos_p, sin_p = _rotary_tables(jnp.arange(s), dk)
    cos_s, sin_s = _rotary_tables(past + jnp.arange(td_pad), dk)
    log_decay = jnp.log1p(-jnp.exp2(-5.0 - jnp.arange(n_ret, dtype=F32)))

    moe_bm = 512
    n_assign = n_tok * TOP_K
    n_blk = -(-n_assign // moe_bm) + n_exp
    n_slot = n_blk * moe_bm
    route_tile = 512
    t_pad = -(-n_tok // route_tile) * route_tile

    xp = x_prompt
    xs = x_sample.reshape(1, n_tok_s, d)
    c_rows = -(-(nb + nbd) // 16) * 16
    c_all = jnp.concatenate([c_prompt, c_sample, jnp.zeros((c_rows - nb - nbd, d), F32)], axis=0)
    kp, vp, sp, ks, vs, ss = [], [], [], [], [], []

    for l in range(depth):
        mods = jnp.split(_adaln(c_all, w_ada[l], b_ada[l]), N_MOD, axis=1)
        mod_p = [m[:nb].reshape(nb, 1, d) for m in mods]
        mod_s = [jnp.repeat(m[nb:nb + nbd], td, axis=0).reshape(1, n_tok_s, d) for m in mods]

        w = w_in[l]
        o1, o2, o3, o4 = sbw, 2 * sbw, 3 * sbw, 3 * sbw + 2 * rqw
        w_q, w_k, w_v = (w[:, a:b].astype(BF16) for a, b in ((0, o1), (o1, o2), (o2, o3)))
        w_qk_r = w[:, o3:o4].astype(BF16)
        w_rest = w[:, o4:].astype(BF16)
        w_sb_up, w_ret_up, w_out = w_up_sb[l].astype(BF16), w_up_ret[l].astype(BF16), w_o[l].astype(BF16)
        w_rt = w_router[l].T.astype(BF16)

        def in_proj(h):
            return (_matmul(h, w_q, BF16), _matmul(h, w_k, F32, BF16), _matmul(h, w_v, F32, BF16),
                    _matmul(h, w_qk_r, F32), _matmul(h, w_rest, BF16))

        h = _norm_mod(xp, g_mix[l], mod_p[0], mod_p[1])
        q_sb, (k_sb, k_bf), (v_sb, v_bf), qk_r, rest = in_proj(h)
        o_sb = _sb_prompt(q_sb, k_bf, v_bf, b_sb[l], n_sb)
        o_norm, state_p = _retention(qk_r, rest, cos_p, sin_p, log_decay, None, n_ret, dk, dv, ret_chunk, ret_chunk)
        merged = _merge(o_sb, o_norm, rest, w_sb_up, w_ret_up)
        xp = _matmul_residual(merged, w_out, xp, mod_p[2])
        kp.append(k_sb.reshape(nb, s, n_sb, dh))
        vp.append(v_sb.reshape(nb, s, n_sb, dh))
        sp.append(state_p)
        h2_p, te_p, tw_p = _norm_router(xp, g_ffn[l], mod_p[3], mod_p[4], w_rt, b_router[l])

        h = _norm_mod(xs, g_mix[l], mod_s[0], mod_s[1])
        q_sb, (k_sb, _), (v_sb, _), qk_r, rest = in_proj(h)
        q_rows = jnp.pad(q_sb.reshape(nbd, td, n_sb, dh).transpose(0, 2, 1, 3),
                         ((0, 0), (0, 0), (0, tq_pad - td), (0, 0))).reshape(nbd, n_sb * tq_pad, dh)
        k_new = jnp.pad(k_sb.reshape(nbd, td, sbw), ((0, 0), (0, page - td), (0, 0))).reshape(nbd, page * n_sb, dh)
        v_new = jnp.pad(v_sb.reshape(nbd, td, sbw), ((0, 0), (0, page - td), (0, 0))).reshape(nbd, page * n_sb, dh)
        bias_rows = jnp.broadcast_to(jnp.repeat(b_sb[l], tq_pad)[:, None], (n_sb * tq_pad, V7X_LANES))
        o_sb = _sb_sample(q_rows, k_new, v_new, cache_sb_k[l].reshape(n_phys, page * n_sb, dh),
                          cache_sb_v[l].reshape(n_phys, page * n_sb, dh), page_table, bias_rows, n_sb, tq_pad)
        o_sb = o_sb.reshape(nbd, n_sb, tq_pad, dh)[:, :, :td].transpose(0, 2, 1, 3)
        o_sb = o_sb.reshape(1, n_tok_s, sbw).astype(BF16)
        pad_t = ((0, 0), (0, td_pad - td), (0, 0))
        qk_pad = jnp.pad(qk_r.reshape(nbd, td, 2 * rqw), pad_t)
        rest_pad = jnp.pad(rest.reshape(nbd, td, rest.shape[-1]), pad_t)
        o_norm, state_s = _retention(qk_pad, rest_pad, cos_s, sin_s, log_decay, state_ret[l], n_ret, dk, dv,
                                     td_pad, td)
        o_norm = o_norm[:, :td].reshape(1, n_tok_s, rvw)
        merged = _merge(o_sb, o_norm, rest, w_sb_up, w_ret_up)
        xs = _matmul_residual(merged, w_out, xs, mod_s[2])
        ks.append(k_sb.reshape(nbd, td, n_sb, dh))
        vs.append(v_sb.reshape(nbd, td, n_sb, dh))
        ss.append(state_s)
        h2_s, te_s, tw_s = _norm_router(xs, g_ffn[l], mod_s[3], mod_s[4], w_rt, b_router[l])

        top_e = jnp.concatenate([te_p.transpose(1, 0, 2).reshape(TOP_K, n_tok_p), te_s.reshape(TOP_K, n_tok_s),
                                 jnp.full((TOP_K, t_pad - n_tok), -1, I32)], axis=1)
        dest, info = _route(top_e, n_exp, moe_bm, n_blk)
        dest = dest.reshape(TOP_K, t_pad // DEST_TILE, DEST_TILE).transpose(1, 0, 2).reshape(-1, TOP_K * DEST_TILE)
        slots = _dispatch(dest, h2_p.reshape(n_tok_p, d), None, n_slot, 0)
        slots = _dispatch(dest, h2_s.reshape(n_tok_s, d), slots, n_slot, n_tok_p // DEST_TILE)
        yb = _moe_gemm(info, slots, w_gate_up[l], b_gate_up[l], w_down[l], b_down[l], moe_bm, n_blk)
        last = l == depth - 1
        xp = _combine(dest, tw_p.transpose(0, 2, 1), xp, mod_p[5], g_final, yb, 0, last)
        xs = _combine(dest, tw_s.transpose(0, 2, 1), xs, mod_s[5], g_final, yb, n_tok_p // DEST_TILE, last)

    return (xp, xs.reshape(nbd, td, d), jnp.stack(kp), jnp.stack(vp), jnp.stack(sp),
            jnp.stack(ks), jnp.stack(vs), jnp.stack(ss))
```

```python
import functools

import jax
import jax.numpy as jnp
from jax import lax
from jax.experimental import pallas as pl
from jax.experimental.pallas import tpu as pltpu

BF16 = jnp.bfloat16
F32 = jnp.float32
I32 = jnp.int32

NORM_EPS = 1e-5
ROPE_BASE = 10000.0
N_MOD = 6
TOP_K = 4
DEST_TILE = 128
SWIGLU_LIMIT = 7.0
SWIGLU_ALPHA = 1.702

V7X_VMEM_BYTES = 64 * 1024 * 1024
V7X_LANES = 128
V7X_SUBLANES = 8
BF16_ROWS = 2 * V7X_SUBLANES
VMEM_CEILING = V7X_VMEM_BYTES * 7 // 8


def _cparams(semantics, vmem_estimate):
    limit = min(max(int(vmem_estimate), 16 * 1024 * 1024), VMEM_CEILING)
    return pltpu.CompilerParams(dimension_semantics=semantics, vmem_limit_bytes=limit)


def _nbytes(shape, dtype):
    n = 1
    for s in shape:
        n *= s
    return n * jnp.dtype(dtype).itemsize


def _tile(n, want):
    t = min(n, want)
    while n % t:
        t -= 1
    return t


def _dot(a, b):
    return jnp.dot(a, b, preferred_element_type=F32)


def _dot_nt(a, b):
    return lax.dot_general(a, b, (((1,), (1,)), ((), ())), preferred_element_type=F32)


def _dot_tn(a, b):
    return lax.dot_general(a, b, (((0,), (0,)), ((), ())), preferred_element_type=F32)


def _sigmoid(x):
    return 1.0 / (1.0 + jnp.exp(-x))


LOG2_E = 1.4426950408889634


def _tri_ones(n):
    row = lax.broadcasted_iota(I32, (n, n + V7X_LANES), 0)
    col = lax.broadcasted_iota(I32, (n, n + V7X_LANES), 1)
    return jnp.where(jnp.logical_or(col >= n, row > col), 1.0, 0.0).astype(BF16)


def _ada_kernel(c_ref, w_ref, b_ref, o_ref):
    c = c_ref[...]
    s = (c * _sigmoid(c)).astype(BF16)
    o_ref[...] = _dot(s, w_ref[...].astype(BF16)) + b_ref[...]


def _adaln(c, w, b):
    m, d = c.shape
    n = w.shape[1]
    tn = _tile(n, 1024)
    est = 2 * (_nbytes((d, tn), F32) + _nbytes((m, tn), F32)) + _nbytes((d, tn), BF16) + _nbytes((m, d), F32) * 4
    return pl.pallas_call(
        _ada_kernel,
        grid=(n // tn,),
        in_specs=[pl.BlockSpec((m, d), lambda j: (0, 0)),
                  pl.BlockSpec((d, tn), lambda j: (0, j)),
                  pl.BlockSpec((1, tn), lambda j: (0, j))],
        out_specs=pl.BlockSpec((m, tn), lambda j: (0, j)),
        out_shape=jax.ShapeDtypeStruct((m, n), F32),
        compiler_params=_cparams(("parallel",), est + (8 << 20)),
        name="adaln",
    )(c, w, b.reshape(1, n))


def _normed(x_ref, g_ref, sh_ref, sc_ref):
    x = x_ref[0]
    y = x * lax.rsqrt(jnp.mean(x * x, axis=-1, keepdims=True) + NORM_EPS) * g_ref[...]
    return y * (1.0 + sc_ref[0]) + sh_ref[0]


def _norm_mod_kernel(x_ref, g_ref, sh_ref, sc_ref, h_ref):
    h_ref[0] = _normed(x_ref, g_ref, sh_ref, sc_ref).astype(h_ref.dtype)


def _norm_router_kernel(x_ref, g_ref, sh_ref, sc_ref, wr_ref, br_ref, h_ref, e_ref, p_ref):
    h = _normed(x_ref, g_ref, sh_ref, sc_ref)
    h_ref[0] = h
    logits = _dot_nt(wr_ref[...], h.astype(BF16)) + br_ref[...]
    n_exp = logits.shape[0]
    iota_e = lax.broadcasted_iota(I32, logits.shape, 0)
    tops = []
    for k in range(TOP_K):
        m = jnp.max(logits, axis=0, keepdims=True)
        idx = jnp.min(jnp.where(logits == m, iota_e, n_exp), axis=0, keepdims=True)
        e_ref[0, k:k + 1, :] = idx
        tops.append(m)
        logits = jnp.where(iota_e == idx, -jnp.inf, logits)
    ex = [jnp.exp(t - tops[0]) for t in tops]
    inv = 1.0 / (ex[0] + ex[1] + ex[2] + ex[3])
    for k in range(TOP_K):
        p_ref[0, k:k + 1, :] = ex[k] * inv


def _mod_spec(mod, ts, d):
    if mod.shape[1] == 1:
        return pl.BlockSpec((1, 1, d), lambda b, i: (b, 0, 0))
    return pl.BlockSpec((1, ts, d), lambda b, i: (b, i, 0))


def _norm_mod(x, g, sh, sc):
    nb, s, d = x.shape
    ts = _tile(s, 512)
    est = 2 * (_nbytes((ts, d), F32) + _nbytes((ts, d), BF16)) + 4 * _nbytes((ts, d), F32)
    return pl.pallas_call(
        _norm_mod_kernel,
        grid=(nb, s // ts),
        in_specs=[pl.BlockSpec((1, ts, d), lambda b, i: (b, i, 0)),
                  pl.BlockSpec((1, d), lambda b, i: (0, 0)),
                  _mod_spec(sh, ts, d), _mod_spec(sc, ts, d)],
        out_specs=pl.BlockSpec((1, ts, d), lambda b, i: (b, i, 0)),
        out_shape=jax.ShapeDtypeStruct((nb, s, d), BF16),
        compiler_params=_cparams(("parallel", "parallel"), est + (8 << 20)),
        name="norm_mod",
    )(x, g.reshape(1, d), sh, sc)


def _norm_router(x, g, sh, sc, w_router_t, b_router):
    nb, s, d = x.shape
    n_exp = w_router_t.shape[0]
    ts = _tile(s, 512)
    est = 4 * _nbytes((ts, d), F32) + 6 * _nbytes((ts, d), F32)
    return pl.pallas_call(
        _norm_router_kernel,
        grid=(nb, s // ts),
        in_specs=[pl.BlockSpec((1, ts, d), lambda b, i: (b, i, 0)),
                  pl.BlockSpec((1, d), lambda b, i: (0, 0)),
                  _mod_spec(sh, ts, d), _mod_spec(sc, ts, d),
                  pl.BlockSpec((n_exp, d), lambda b, i: (0, 0)),
                  pl.BlockSpec((n_exp, 1), lambda b, i: (0, 0))],
        out_specs=[pl.BlockSpec((1, ts, d), lambda b, i: (b, i, 0)),
                   pl.BlockSpec((1, TOP_K, ts), lambda b, i: (b, 0, i)),
                   pl.BlockSpec((1, TOP_K, ts), lambda b, i: (b, 0, i))],
        out_shape=[jax.ShapeDtypeStruct((nb, s, d), F32),
                   jax.ShapeDtypeStruct((nb, TOP_K, s), I32),
                   jax.ShapeDtypeStruct((nb, TOP_K, s), F32)],
        compiler_params=_cparams(("parallel", "parallel"), est + (8 << 20)),
        name="norm_router",
    )(x, g.reshape(1, d), sh, sc, w_router_t, b_router.reshape(n_exp, 1))


def _mm_kernel(a_ref, w_ref, *o_refs):
    acc = _dot(a_ref[0], w_ref[...])
    for o_ref in o_refs:
        o_ref[0] = acc.astype(o_ref.dtype)


def _mm_res_kernel(a_ref, w_ref, res_ref, gate_ref, o_ref):
    o_ref[0] = res_ref[0] + gate_ref[0] * _dot(a_ref[0], w_ref[...])


def _mm_tiles(s, k, n):
    return _tile(s, 1024), _tile(n, 1024)


def _matmul(a, w, *out_dtypes):
    nb, s, k = a.shape
    n = w.shape[1]
    tm, tn = _mm_tiles(s, k, n)
    est = 2 * (_nbytes((tm, k), BF16) + _nbytes((k, tn), BF16) + 2 * _nbytes((tm, tn), F32)) + _nbytes((tm, tn), F32)
    outs = pl.pallas_call(
        _mm_kernel,
        grid=(nb, s // tm, n // tn),
        in_specs=[pl.BlockSpec((1, tm, k), lambda b, i, j: (b, i, 0)),
                  pl.BlockSpec((k, tn), lambda b, i, j: (0, j))],
        out_specs=[pl.BlockSpec((1, tm, tn), lambda b, i, j: (b, i, j)) for _ in out_dtypes],
        out_shape=[jax.ShapeDtypeStruct((nb, s, n), dt) for dt in out_dtypes],
        compiler_params=_cparams(("parallel", "parallel", "parallel"), est + (8 << 20)),
        name="matmul",
    )(a, w)
    return outs[0] if len(outs) == 1 else outs


def _matmul_residual(a, w, res, gate):
    nb, s, k = a.shape
    n = w.shape[1]
    tm, tn = _mm_tiles(s, k, n)
    if gate.shape[1] == 1:
        gate_spec = pl.BlockSpec((1, 1, tn), lambda b, i, j: (b, 0, j))
    else:
        gate_spec = pl.BlockSpec((1, tm, tn), lambda b, i, j: (b, i, j))
    est = 2 * (_nbytes((tm, k), BF16) + _nbytes((k, tn), BF16) + 3 * _nbytes((tm, tn), F32)) + _nbytes((tm, tn), F32)
    return pl.pallas_call(
        _mm_res_kernel,
        grid=(nb, s // tm, n // tn),
        in_specs=[pl.BlockSpec((1, tm, k), lambda b, i, j: (b, i, 0)),
                  pl.BlockSpec((k, tn), lambda b, i, j: (0, j)),
                  pl.BlockSpec((1, tm, tn), lambda b, i, j: (b, i, j)),
                  gate_spec],
        out_specs=pl.BlockSpec((1, tm, tn), lambda b, i, j: (b, i, j)),
        out_shape=jax.ShapeDtypeStruct((nb, s, n), F32),
        compiler_params=_cparams(("parallel", "parallel", "parallel"), est + (8 << 20)),
        name="matmul_residual",
    )(a, w, res, gate)


def _sb_block(q, k, v, bias, scale, carry, tri_ones, mask):
    a, carry = _sb_weights(_dot_nt(q, k) * scale + bias, carry, tri_ones, mask)
    return _dot(a, v), carry


def _sb_weights(z2, carry, tri_ones, mask):
    bk = z2.shape[1]
    soft = jnp.log2(1.0 + jnp.exp2(-jnp.abs(z2)))
    lb = jnp.minimum(z2, 0.0) - soft
    lk = lb - z2
    if mask is not None:
        lk = jnp.where(mask, lk, 0.0)
    cs = _dot(lk.astype(BF16), tri_ones)
    between = cs[:, :bk] + jnp.concatenate([carry] * (bk // V7X_LANES), axis=1)
    a = jnp.exp2(lb + between)
    if mask is not None:
        a = jnp.where(mask, a, 0.0)
    return a.astype(BF16), carry + cs[:, bk:]


def _sbp_kernel(bias_ref, q_ref, k_ref, v_ref, o_ref, acc_ref, carry_ref, *, blk, scale, group, dh):
    qi = pl.program_id(2)
    biases = [bias_ref[pl.program_id(1) * group + g] * LOG2_E for g in range(group)]
    tri_ones = _tri_ones(blk)
    row = lax.broadcasted_iota(I32, (blk, blk), 0)
    col = lax.broadcasted_iota(I32, (blk, blk), 1)

    def sweep(start, carries, mask):
        outs, new = [], []
        for g in range(group):
            cols = slice(g * dh, (g + 1) * dh)
            out, carry = _sb_block(q_ref[0, :, cols], k_ref[0, pl.ds(start, blk), cols],
                                   v_ref[0, pl.ds(start, blk), cols], biases[g], scale, carries[g], tri_ones, mask)
            outs.append(out)
            new.append(carry)
        return jnp.concatenate(outs, axis=1), jnp.concatenate(new, axis=1)

    zero = jnp.zeros((blk, V7X_LANES), F32)
    out, carry = sweep(pl.multiple_of(qi * blk, blk), [zero] * group, col < row)
    acc_ref[...] = out
    carry_ref[...] = carry

    def body(j, _):
        carry = carry_ref[...]
        out, carry = sweep(pl.multiple_of((qi - 1 - j) * blk, blk),
                           [carry[:, g * V7X_LANES:(g + 1) * V7X_LANES] for g in range(group)], None)
        acc_ref[...] += out
        carry_ref[...] = carry
        return 0

    lax.fori_loop(0, qi, body, 0)
    o_ref[0] = acc_ref[...].astype(o_ref.dtype)


def _sb_prompt(q, k, v, bias, n_heads):
    nb, s, w = q.shape
    dh = w // n_heads
    blk = _tile(s, 256)
    group = _tile(n_heads, 8)
    gw = group * dh
    est = 8 * _nbytes((s, gw), BF16) + 4 * _nbytes((blk, gw), F32) + 12 * group * _nbytes((blk, blk), F32)
    return pl.pallas_call(
        functools.partial(_sbp_kernel, blk=blk, scale=dh ** -0.5 * LOG2_E, group=group, dh=dh),
        grid=(nb, n_heads // group, s // blk),
        in_specs=[pl.BlockSpec(memory_space=pltpu.SMEM),
                  pl.BlockSpec((1, blk, gw), lambda b, h, i: (b, i, h)),
                  pl.BlockSpec((1, s, gw), lambda b, h, i: (b, 0, h)),
                  pl.BlockSpec((1, s, gw), lambda b, h, i: (b, 0, h))],
        out_specs=pl.BlockSpec((1, blk, gw), lambda b, h, i: (b, i, h)),
        scratch_shapes=[pltpu.VMEM((blk, gw), F32), pltpu.VMEM((blk, group * V7X_LANES), F32)],
        out_shape=jax.ShapeDtypeStruct((nb, s, w), BF16),
        compiler_params=_cparams(("parallel", "parallel", "arbitrary"), est + (8 << 20)),
        name="sb_prompt",
    )(bias, q, k, v)


def _sbs_kernel(pt_ref, bias_ref, q_ref, kn_ref, vn_ref, *refs, pages_per_step, pages_per_group, rows_per_head,
                n_heads, page, scale):
    del pt_ref
    k_refs = refs[:pages_per_step]
    v_refs = refs[pages_per_step:2 * pages_per_step]
    o_ref, acc_ref, carry_ref = refs[2 * pages_per_step:]
    s = pl.program_id(1)
    rph = rows_per_head
    qs = [q_ref[0, h * rph:(h + 1) * rph, :] for h in range(n_heads)]
    bias = bias_ref[...] * LOG2_E

    def head_rows(page_refs, h):
        parts = [ref[0, pl.ds(h, page, stride=n_heads), :].astype(BF16) for ref in page_refs]
        return parts[0] if len(parts) == 1 else jnp.concatenate(parts, axis=0)

    def sweep(groups, mask):
        carry = carry_ref[...]
        total = None
        for k_pages, v_pages in groups:
            n = len(k_pages)
            z = jnp.concatenate([_dot_nt(qs[h], head_rows(k_pages, h)) for h in range(n_heads)], axis=0)
            z = z * scale + jnp.concatenate([bias] * n, axis=1)
            a, carry = _sb_weights(z, carry, _tri_ones(n * page), mask)
            out = jnp.concatenate([_dot(a[h * rph:(h + 1) * rph], head_rows(v_pages, h)) for h in range(n_heads)],
                                  axis=0)
            total = out if total is None else total + out
        acc_ref[...] += total
        carry_ref[...] = carry

    @pl.when(s == 0)
    def _():
        acc_ref[...] = jnp.zeros_like(acc_ref)
        carry_ref[...] = jnp.zeros_like(carry_ref)
        row = lax.broadcasted_iota(I32, (n_heads * rph, page), 0)
        col = lax.broadcasted_iota(I32, (n_heads * rph, page), 1)
        sweep([([kn_ref], [vn_ref])], col < row % rph)

    groups = [(list(k_refs[g:g + pages_per_group])[::-1], list(v_refs[g:g + pages_per_group])[::-1])
              for g in range(0, pages_per_step, pages_per_group)]
    sweep(groups, None)

    @pl.when(s == pl.num_programs(1) - 1)
    def _():
        o_ref[0] = acc_ref[...]


def _sb_sample(q, k_new, v_new, cache_k, cache_v, page_table, bias_rows, n_heads, rows_per_head):
    nbd, r, dh = q.shape
    page = cache_k.shape[1] // n_heads
    n_pages = page_table.shape[1]
    pps = _tile(n_pages, 8)
    n_steps = n_pages // pps
    blk = (1, page * n_heads, dh)

    def page_spec(p):
        return pl.BlockSpec(blk, lambda b, s, pt: (pt[b, n_pages - 1 - (s * pps + p)], 0, 0))

    in_specs = [pl.BlockSpec((r, V7X_LANES), lambda b, s, pt: (0, 0)),
                pl.BlockSpec((1, r, dh), lambda b, s, pt: (b, 0, 0)),
                pl.BlockSpec(blk, lambda b, s, pt: (b, 0, 0)),
                pl.BlockSpec(blk, lambda b, s, pt: (b, 0, 0))]
    in_specs += [page_spec(p) for p in range(pps)] * 2
    est = 2 * (2 * pps + 2) * _nbytes(blk, F32) + 16 * _nbytes((r, page), F32) + 4 * _nbytes(blk, BF16)
    return pl.pallas_call(
        functools.partial(_sbs_kernel, pages_per_step=pps, pages_per_group=_tile(pps, 2),
                          rows_per_head=rows_per_head, n_heads=n_heads,
                          page=page, scale=dh ** -0.5 * LOG2_E),
        grid_spec=pltpu.PrefetchScalarGridSpec(
            num_scalar_prefetch=1,
            grid=(nbd, n_steps),
            in_specs=in_specs,
            out_specs=pl.BlockSpec((1, r, dh), lambda b, s, pt: (b, 0, 0)),
            scratch_shapes=[pltpu.VMEM((r, dh), F32), pltpu.VMEM((r, V7X_LANES), F32)]),
        out_shape=jax.ShapeDtypeStruct((nbd, r, dh), F32),
        compiler_params=_cparams(("parallel", "arbitrary"), est + (8 << 20)),
        name="sb_sample",
    )(page_table, bias_rows, q, k_new, v_new, *([cache_k] * pps), *([cache_v] * pps))


def _ret_kernel(lg_ref, q_ref, k_ref, v_ref, cos_ref, sin_ref, *refs, c_true, has_init, kscale, n_heads, dv):
    if has_init:
        s0_ref, o_ref, sout_ref, st_ref = refs
    else:
        o_ref, sout_ref, st_ref = refs
    c = pl.program_id(1)

    @pl.when(c == 0)
    def _():
        if has_init:
            st_ref[...] = s0_ref[0]
        else:
            st_ref[...] = jnp.zeros_like(st_ref)

    cos = cos_ref[...]
    sin = sin_ref[...]
    n_rows, dk = cos.shape

    def rot(x):
        return x * cos + pltpu.roll(x, dk // 2, 1) * sin

    n = lax.broadcasted_iota(I32, (n_rows, n_rows), 0)
    m = lax.broadcasted_iota(I32, (n_rows, n_rows), 1)
    diff = (n - m).astype(F32)
    pos = lax.broadcasted_iota(I32, (n_rows, dk), 0).astype(F32)
    outs = []
    for h in range(n_heads):
        lg = lg_ref[h]
        q = rot(q_ref[0, :, h * dk:(h + 1) * dk])
        k = rot(k_ref[0, :, h * dk:(h + 1) * dk]) * kscale
        v = v_ref[0, :, h * dv:(h + 1) * dv]
        decay = jnp.where(diff >= 0.0, jnp.exp(jnp.maximum(diff, 0.0) * lg), 0.0)
        scores = _dot_nt(q.astype(BF16), k.astype(BF16)) * decay
        q_dec = q * jnp.exp((pos + 1.0) * lg)
        k_dec = k * jnp.exp((c_true - 1.0 - pos) * lg)
        st = st_ref[h]
        o = _dot(scores.astype(BF16), v) + _dot(q_dec.astype(BF16), st.astype(BF16))
        st_ref[h] = jnp.exp(jnp.full((1, dv), c_true * lg, F32)) * st + _dot_tn(k_dec.astype(BF16), v)
        mu = jnp.mean(o, axis=-1, keepdims=True)
        dev = o - mu
        var = jnp.mean(dev * dev, axis=-1, keepdims=True)
        outs.append((dev * lax.rsqrt(var + NORM_EPS)).astype(o_ref.dtype))
    o_ref[0] = jnp.concatenate(outs, axis=1)

    @pl.when(c == pl.num_programs(1) - 1)
    def _():
        sout_ref[0] = st_ref[...]


def _retention(qk, v_src, cos2, sin2, log_decay, state0, n_heads, dk, dv, chunk, c_true):
    nb, s, _ = qk.shape
    nc = s // chunk
    qw, vw = n_heads * dk, n_heads * dv
    in_specs = [pl.BlockSpec(memory_space=pltpu.SMEM),
                pl.BlockSpec((1, chunk, qw), lambda b, c: (b, c, 0)),
                pl.BlockSpec((1, chunk, qw), lambda b, c: (b, c, 1)),
                pl.BlockSpec((1, chunk, vw), lambda b, c: (b, c, 0)),
                pl.BlockSpec((chunk, dk), lambda b, c: (c, 0)),
                pl.BlockSpec((chunk, dk), lambda b, c: (c, 0))]
    args = [log_decay, qk, qk, v_src, cos2, sin2]
    if state0 is not None:
        in_specs.append(pl.BlockSpec((1, n_heads, dk, dv), lambda b, c: (b, 0, 0, 0)))
        args.append(state0)
    est = (7 * _nbytes((n_heads, dk, dv), F32) + 4 * _nbytes((chunk, qw), F32) + 4 * _nbytes((chunk, vw), BF16)
           + n_heads * (8 * _nbytes((chunk, dv), F32) + 6 * _nbytes((chunk, chunk), F32)))
    return pl.pallas_call(
        functools.partial(_ret_kernel, c_true=float(c_true), has_init=state0 is not None, kscale=dk ** -0.5,
                          n_heads=n_heads, dv=dv),
        grid=(nb, nc),
        in_specs=in_specs,
        out_specs=[pl.BlockSpec((1, chunk, vw), lambda b, c: (b, c, 0)),
                   pl.BlockSpec((1, n_heads, dk, dv), lambda b, c: (b, 0, 0, 0))],
        out_shape=[jax.ShapeDtypeStruct((nb, s, vw), BF16),
                   jax.ShapeDtypeStruct((nb, n_heads, dk, dv), F32)],
        scratch_shapes=[pltpu.VMEM((n_heads, dk, dv), F32)],
        compiler_params=_cparams(("parallel", "arbitrary"), est + (8 << 20)),
        name="retention",
    )(*args)


def _merge_kernel(osb_ref, on_ref, g_ref, asb_ref, ar_ref, wsb_ref, wret_ref, o_ref):
    y_sb = _dot(osb_ref[0], wsb_ref[...])
    g = g_ref[0].astype(F32)
    u = (g * _sigmoid(g)) * on_ref[0].astype(F32)
    y_r = _dot(u.astype(BF16), wret_ref[...])
    merged = _sigmoid(asb_ref[0].astype(F32)) * y_sb + _sigmoid(ar_ref[0].astype(F32)) * y_r
    o_ref[0] = merged.astype(o_ref.dtype)


def _merge(o_sb, o_norm, rest, w_up_sb, w_up_ret):
    nb, s, sbw = o_sb.shape
    rvw = o_norm.shape[2]
    d = w_up_sb.shape[1]
    tm = _tile(s, 256)
    est = (2 * (_nbytes((sbw, d), BF16) + _nbytes((rvw, d), BF16))
           + 2 * (_nbytes((tm, sbw), BF16) + 5 * _nbytes((tm, d), BF16)) + 8 * _nbytes((tm, d), F32))
    return pl.pallas_call(
        _merge_kernel,
        grid=(nb, s // tm),
        in_specs=[pl.BlockSpec((1, tm, sbw), lambda b, i: (b, i, 0)),
                  pl.BlockSpec((1, tm, rvw), lambda b, i: (b, i, 0)),
                  pl.BlockSpec((1, tm, rvw), lambda b, i: (b, i, 1)),
                  pl.BlockSpec((1, tm, d), lambda b, i: (b, i, 2)),
                  pl.BlockSpec((1, tm, d), lambda b, i: (b, i, 3)),
                  pl.BlockSpec((sbw, d), lambda b, i: (0, 0)),
                  pl.BlockSpec((rvw, d), lambda b, i: (0, 0))],
        out_specs=pl.BlockSpec((1, tm, d), lambda b, i: (b, i, 0)),
        out_shape=jax.ShapeDtypeStruct((nb, s, d), BF16),
        compiler_params=_cparams(("parallel", "parallel"), est + (8 << 20)),
        name="merge",
    )(o_sb, o_norm, rest, rest, rest, w_up_sb, w_up_ret)


def _route_kernel(te_ref, dest_ref, info_ref, cnt_ref, base_ref, *, n_exp, bm, n_info):
    phase = pl.program_id(0)
    i = pl.program_id(1)
    te = te_ref[...]
    tm = te.shape[1]
    iota_e = lax.broadcasted_iota(I32, (n_exp, tm), 0)
    ones = jnp.ones((tm, V7X_LANES), BF16)
    onehots = [jnp.where(iota_e == te[k:k + 1, :], 1.0, 0.0).astype(BF16) for k in range(TOP_K)]

    @pl.when(jnp.logical_and(phase == 0, i == 0))
    def _():
        cnt_ref[...] = jnp.zeros_like(cnt_ref)

    @pl.when(phase == 0)
    def _():
        cnt_ref[...] += _dot(onehots[0] + onehots[1] + onehots[2] + onehots[3], ones)

    @pl.when(jnp.logical_and(phase == 1, i == 0))
    def _():
        counts = cnt_ref[...]
        n_blocks = jnp.floor((counts + (bm - 0.5)) * (1.0 / bm))
        padded = n_blocks * bm
        er = lax.broadcasted_iota(I32, (n_exp, n_exp), 0)
        ec = lax.broadcasted_iota(I32, (n_exp, n_exp), 1)
        lower = jnp.where(ec < er, 1.0, 0.0).astype(BF16)
        hi = jnp.floor(n_blocks * (1.0 / 256.0))
        lo = n_blocks - 256.0 * hi
        pstart = (256.0 * _dot(lower, hi.astype(BF16)) + _dot(lower, lo.astype(BF16))) * bm
        base_ref[...] = pstart
        reps = n_info // V7X_LANES
        pend = jnp.concatenate([pstart + padded] * reps, axis=1)
        used = jnp.concatenate([pstart + counts] * reps, axis=1)
        blk_start = (lax.broadcasted_iota(I32, (n_exp, n_info), 1) * bm).astype(F32)
        blk_e = jnp.minimum(jnp.sum(jnp.where(pend <= blk_start, 1.0, 0.0), axis=0, keepdims=True), n_exp - 1.0)
        mine = lax.broadcasted_iota(I32, (n_exp, n_info), 0).astype(F32) == blk_e
        valid = jnp.sum(jnp.where(mine, used - blk_start, 0.0), axis=0, keepdims=True)
        valid = jnp.clip(valid, 0.0, float(bm))
        n_active = jnp.floor((jnp.sum(padded[:, :1], axis=0, keepdims=True) + 0.5) * (1.0 / bm))
        info_ref[...] = jnp.zeros_like(info_ref)
        info_ref[0:1, :] = blk_e.astype(I32)
        info_ref[1:2, :] = valid.astype(I32)
        info_ref[2:3, :] = jnp.broadcast_to(n_active, (1, n_info)).astype(I32)

    @pl.when(phase == 1)
    def _():
        tri = jnp.where(lax.broadcasted_iota(I32, (tm, tm), 0) < lax.broadcasted_iota(I32, (tm, tm), 1),
                        1.0, 0.0).astype(BF16)
        base = base_ref[...]
        for k in range(TOP_K):
            before = _dot(onehots[k], tri) + jnp.concatenate([base] * (tm // V7X_LANES), axis=1)
            dest = jnp.sum(jnp.where(onehots[k] > 0, before, 0.0), axis=0, keepdims=True)
            dest_ref[k:k + 1, :] = dest.astype(I32)
            base = base + _dot(onehots[k], ones)
        base_ref[...] = base


def _route(top_e, n_exp, bm, n_blk):
    _, tpad = top_e.shape
    tm = _tile(tpad, 512)
    n_info = -(-n_blk // V7X_LANES) * V7X_LANES
    est = 8 * _nbytes((tm, tm), F32) + 16 * _nbytes((n_exp, tm), F32)
    return pl.pallas_call(
        functools.partial(_route_kernel, n_exp=n_exp, bm=bm, n_info=n_info),
        grid=(2, tpad // tm),
        in_specs=[pl.BlockSpec((TOP_K, tm), lambda p, i: (0, i))],
        out_specs=[pl.BlockSpec((TOP_K, tm), lambda p, i: (0, i * p)),
                   pl.BlockSpec((V7X_SUBLANES, n_info), lambda p, i: (0, 0))],
        out_shape=[jax.ShapeDtypeStruct((TOP_K, tpad), I32),
                   jax.ShapeDtypeStruct((V7X_SUBLANES, n_info), I32)],
        scratch_shapes=[pltpu.VMEM((n_exp, V7X_LANES), F32), pltpu.VMEM((n_exp, V7X_LANES), F32)],
        compiler_params=_cparams(("arbitrary", "arbitrary"), est + (8 << 20)),
        name="moe_route",
    )(top_e)


def _row_copy(src_ref, src_row, dst_ref, dst_row, sem):
    return pltpu.make_async_copy(src_ref.at[pl.ds(src_row, 1)], dst_ref.at[pl.ds(dst_row, 1)], sem)


def _dispatch_kernel(dest_ref, h_ref, *refs, rows, first_tile):
    xs_ref, sem = refs[-2:]
    tile = first_tile + pl.program_id(0)

    def issue(r, _):
        for k in range(TOP_K):
            _row_copy(h_ref, r, xs_ref, dest_ref[tile, k * rows + r], sem).start()
        return 0

    lax.fori_loop(0, rows, issue, 0, unroll=8)

    def drain(r, _):
        for k in range(TOP_K):
            _row_copy(h_ref, 0, xs_ref, 0, sem).wait()
        return 0

    lax.fori_loop(0, rows, drain, 0, unroll=True)


def _dispatch(dest, h, xs, n_slot, first_tile):
    t, d = h.shape
    rows = DEST_TILE
    assert t % rows == 0
    in_specs = [pl.BlockSpec((rows, d), lambda i, dest: (i, 0))]
    args = [dest, h]
    aliases = {}
    if xs is not None:
        in_specs.append(pl.BlockSpec(memory_space=pl.ANY))
        args.append(xs)
        aliases = {2: 0}
    return pl.pallas_call(
        functools.partial(_dispatch_kernel, rows=rows, first_tile=first_tile),
        grid_spec=pltpu.PrefetchScalarGridSpec(
            num_scalar_prefetch=1,
            grid=(t // rows,),
            in_specs=in_specs,
            out_specs=pl.BlockSpec(memory_space=pl.ANY),
            scratch_shapes=[pltpu.SemaphoreType.DMA]),
        out_shape=jax.ShapeDtypeStruct((n_slot, d), F32),
        input_output_aliases=aliases,
        compiler_params=_cparams(("arbitrary",), 4 * _nbytes((rows, d), F32) + (8 << 20)),
        name="moe_dispatch",
    )(*args)


def _moe_gemm_kernel(info_ref, x_ref, wg_ref, wu_ref, bg_ref, bu_ref, wd_ref, bd_ref, o_ref, xb_ref):
    i = pl.program_id(0)
    j = pl.program_id(1)

    @pl.when(i < info_ref[2, 0])
    def _():
        @pl.when(j == 0)
        def _():
            row = lax.broadcasted_iota(I32, x_ref.shape, 0)
            xb_ref[...] = jnp.where(row < info_ref[1, i], x_ref[...], 0.0).astype(BF16)

        xb = xb_ref[...]
        gate = jnp.minimum(_dot(xb, wg_ref[0].astype(BF16)) + bg_ref[0], SWIGLU_LIMIT)
        up = jnp.clip(_dot(xb, wu_ref[0].astype(BF16)) + bu_ref[0], -SWIGLU_LIMIT, SWIGLU_LIMIT)
        act = (up + 1.0) * gate * _sigmoid(SWIGLU_ALPHA * gate)
        part = _dot(act.astype(BF16), wd_ref[0].astype(BF16))

        @pl.when(j == 0)
        def _():
            o_ref[...] = part + bd_ref[0]

        @pl.when(j > 0)
        def _():
            o_ref[...] += part


def _moe_gemm(info, xs, w_gate_up, b_gate_up, w_down, b_down, bm, n_blk):
    n_slot, d = xs.shape
    n_exp, _, f2 = w_gate_up.shape
    f = f2 // 2
    tf = _tile(f, 512)
    nf = f // tf

    def blk(i, info):
        return jnp.minimum(i, info[2, 0] - 1)

    def ff(i, j, info):
        return jnp.where(i < info[2, 0], j, nf - 1)

    in_specs = [pl.BlockSpec((bm, d), lambda i, j, info: (blk(i, info), 0)),
                pl.BlockSpec((1, d, tf), lambda i, j, info: (info[0, blk(i, info)], 0, ff(i, j, info))),
                pl.BlockSpec((1, d, tf), lambda i, j, info: (info[0, blk(i, info)], 0, nf + ff(i, j, info))),
                pl.BlockSpec((1, 1, tf), lambda i, j, info: (info[0, blk(i, info)], 0, ff(i, j, info))),
                pl.BlockSpec((1, 1, tf), lambda i, j, info: (info[0, blk(i, info)], 0, nf + ff(i, j, info))),
                pl.BlockSpec((1, tf, d), lambda i, j, info: (info[0, blk(i, info)], ff(i, j, info), 0)),
                pl.BlockSpec((1, 1, d), lambda i, j, info: (info[0, blk(i, info)], 0, 0))]
    est = (2 * (2 * _nbytes((bm, d), F32) + 3 * _nbytes((d, tf), F32)) + _nbytes((bm, d), BF16)
           + 3 * _nbytes((d, tf), BF16) + 6 * _nbytes((bm, tf), F32))
    return pl.pallas_call(
        _moe_gemm_kernel,
        grid_spec=pltpu.PrefetchScalarGridSpec(
            num_scalar_prefetch=1,
            grid=(n_blk, nf),
            in_specs=in_specs,
            out_specs=pl.BlockSpec((bm, d), lambda i, j, info: (blk(i, info), 0)),
            scratch_shapes=[pltpu.VMEM((bm, d), BF16)]),
        out_shape=jax.ShapeDtypeStruct((n_slot, d), F32),
        compiler_params=_cparams(("arbitrary", "arbitrary"), est + (8 << 20)),
        name="moe_gemm",
    )(info, xs, w_gate_up, w_gate_up, b_gate_up.reshape(n_exp, 1, f2), b_gate_up.reshape(n_exp, 1, f2),
      w_down, b_down.reshape(n_exp, 1, d))


def _combine_kernel(dest_ref, p_ref, res_ref, gate_ref, g_ref, yb_ref, o_ref, buf_ref, sem, *, rows, first_tile,
                    final_norm):
    tile = first_tile + pl.program_id(0) * pl.num_programs(1) + pl.program_id(1)

    def issue(r, _):
        for k in range(TOP_K):
            _row_copy(yb_ref, dest_ref[tile, k * rows + r], buf_ref.at[k], r, sem).start()
        return 0

    lax.fori_loop(0, rows, issue, 0, unroll=8)

    def drain(r, _):
        for k in range(TOP_K):
            _row_copy(yb_ref, 0, buf_ref.at[k], 0, sem).wait()
        return 0

    lax.fori_loop(0, rows, drain, 0, unroll=True)
    p = p_ref[0]
    y = p[:, 0:1] * buf_ref[0]
    for k in range(1, TOP_K):
        y = y + p[:, k:k + 1] * buf_ref[k]
    x = res_ref[0] + gate_ref[0] * y
    if final_norm:
        x = x * lax.rsqrt(jnp.mean(x * x, axis=-1, keepdims=True) + NORM_EPS) * g_ref[...]
    o_ref[0] = x


def _combine(dest, probs, res, gate, g_final, yb, first_tile, final_norm):
    nb, s, d = res.shape
    rows = DEST_TILE
    assert s % rows == 0
    if gate.shape[1] == 1:
        gate_spec = pl.BlockSpec((1, 1, d), lambda b, i, dest: (b, 0, 0))
    else:
        gate_spec = pl.BlockSpec((1, rows, d), lambda b, i, dest: (b, i, 0))
    est = _nbytes((TOP_K, rows, d), F32) + 6 * _nbytes((rows, d), F32) + 4 * _nbytes((rows, d), F32)
    return pl.pallas_call(
        functools.partial(_combine_kernel, rows=rows, first_tile=first_tile, final_norm=final_norm),
        grid_spec=pltpu.PrefetchScalarGridSpec(
            num_scalar_prefetch=1,
            grid=(nb, s // rows),
            in_specs=[pl.BlockSpec((1, rows, TOP_K), lambda b, i, dest: (b, i, 0)),
                      pl.BlockSpec((1, rows, d), lambda b, i, dest: (b, i, 0)),
                      gate_spec,
                      pl.BlockSpec((1, d), lambda b, i, dest: (0, 0)),
                      pl.BlockSpec(memory_space=pl.ANY)],
            out_specs=pl.BlockSpec((1, rows, d), lambda b, i, dest: (b, i, 0)),
            scratch_shapes=[pltpu.VMEM((TOP_K, rows, d), F32), pltpu.SemaphoreType.DMA]),
        out_shape=jax.ShapeDtypeStruct((nb, s, d), F32),
        compiler_params=_cparams(("arbitrary", "arbitrary"), est + (8 << 20)),
        name="moe_combine",
    )(dest, probs, res, gate, g_final.reshape(1, d), yb)


def _rotary_tables(pos, dk):
    half = dk // 2
    inv_freq = ROPE_BASE ** (-jnp.arange(half, dtype=F32) / half)
    ang = pos.astype(F32)[:, None] * inv_freq[None, :]
    cos, sin = jnp.cos(ang), jnp.sin(ang)
    return jnp.concatenate([cos, cos], axis=1), jnp.concatenate([-sin, sin], axis=1)


def kernel(x_prompt, x_sample, cache_sb_k, cache_sb_v, state_ret, page_table, c_prompt, c_sample,
           w_ada, b_ada, g_mix, w_in, b_sb, w_up_sb, w_up_ret, w_o, g_ffn, w_router, b_router,
           w_gate_up, b_gate_up, w_down, b_down, g_final):
    nb, s, d = x_prompt.shape
    nbd, td, _ = x_sample.shape
    depth, n_phys, page, n_sb, dh = cache_sb_k.shape
    _, _, n_ret, dk, dv = state_ret.shape
    n_pages = page_table.shape[1]
    past = n_pages * page
    sbw, rqw, rvw = n_sb * dh, n_ret * dk, n_ret * dv
    n_exp = w_router.shape[-1]
    assert rvw == d and w_in.shape[-1] == 3 * sbw + 2 * rqw + 2 * rvw + 2 * d
    n_tok_p, n_tok_s = nb * s, nbd * td
    n_tok = n_tok_p + n_tok_s

    ret_chunk = _tile(s, 128)
    td_pad = -(-td // V7X_SUBLANES) * V7X_SUBLANES
    tq_pad = -(-td // BF16_ROWS) * BF16_ROWS
    cos_p, sin_p = _rotary_tables(jnp.arange(s), dk)
    cos_s, sin_s = _rotary_tables(past + jnp.arange(td_pad), dk)
    log_decay = jnp.log1p(-jnp.exp2(-5.0 - jnp.arange(n_ret, dtype=F32)))

    moe_bm = 640
    n_assign = n_tok * TOP_K
    n_blk = -(-n_assign // moe_bm) + n_exp
    n_slot = n_blk * moe_bm
    route_tile = 512
    t_pad = -(-n_tok // route_tile) * route_tile

    xp = x_prompt
    xs = x_sample.reshape(1, n_tok_s, d)
    c_rows = -(-(nb + nbd) // 16) * 16
    c_all = jnp.concatenate([c_prompt, c_sample, jnp.zeros((c_rows - nb - nbd, d), F32)], axis=0)
    kp, vp, sp, ks, vs, ss = [], [], [], [], [], []

    for l in range(depth):
        mods = jnp.split(_adaln(c_all, w_ada[l], b_ada[l]), N_MOD, axis=1)
        mod_p = [m[:nb].reshape(nb, 1, d) for m in mods]
        mod_s = [jnp.repeat(m[nb:nb + nbd], td, axis=0).reshape(1, n_tok_s, d) for m in mods]

        w = w_in[l]
        o1, o2, o3, o4 = sbw, 2 * sbw, 3 * sbw, 3 * sbw + 2 * rqw
        w_q, w_k, w_v = (w[:, a:b].astype(BF16) for a, b in ((0, o1), (o1, o2), (o2, o3)))
        w_qk_r = w[:, o3:o4].astype(BF16)
        w_rest = w[:, o4:].astype(BF16)
        w_sb_up, w_ret_up, w_out = w_up_sb[l].astype(BF16), w_up_ret[l].astype(BF16), w_o[l].astype(BF16)
        w_rt = w_router[l].T.astype(BF16)

        def in_proj(h):
            return (_matmul(h, w_q, BF16), _matmul(h, w_k, F32, BF16), _matmul(h, w_v, F32, BF16),
                    _matmul(h, w_qk_r, F32), _matmul(h, w_rest, BF16))

        h = _norm_mod(xp, g_mix[l], mod_p[0], mod_p[1])
        q_sb, (k_sb, k_bf), (v_sb, v_bf), qk_r, rest = in_proj(h)
        o_sb = _sb_prompt(q_sb, k_bf, v_bf, b_sb[l], n_sb)
        o_norm, state_p = _retention(qk_r, rest, cos_p, sin_p, log_decay, None, n_ret, dk, dv, ret_chunk, ret_chunk)
        merged = _merge(o_sb, o_norm, rest, w_sb_up, w_ret_up)
        xp = _matmul_residual(merged, w_out, xp, mod_p[2])
        kp.append(k_sb.reshape(nb, s, n_sb, dh))
        vp.append(v_sb.reshape(nb, s, n_sb, dh))
        sp.append(state_p)
        h2_p, te_p, tw_p = _norm_router(xp, g_ffn[l], mod_p[3], mod_p[4], w_rt, b_router[l])

        h = _norm_mod(xs, g_mix[l], mod_s[0], mod_s[1])
        q_sb, (k_sb, _), (v_sb, _), qk_r, rest = in_proj(h)
        q_rows = jnp.pad(q_sb.reshape(nbd, td, n_sb, dh).transpose(0, 2, 1, 3),
                         ((0, 0), (0, 0), (0, tq_pad - td), (0, 0))).reshape(nbd, n_sb * tq_pad, dh)
        k_new = jnp.pad(k_sb.reshape(nbd, td, sbw), ((0, 0), (0, page - td), (0, 0))).reshape(nbd, page * n_sb, dh)
        v_new = jnp.pad(v_sb.reshape(nbd, td, sbw), ((0, 0), (0, page - td), (0, 0))).reshape(nbd, page * n_sb, dh)
        bias_rows = jnp.broadcast_to(jnp.repeat(b_sb[l], tq_pad)[:, None], (n_sb * tq_pad, V7X_LANES))
        o_sb = _sb_sample(q_rows, k_new, v_new, cache_sb_k[l].reshape(n_phys, page * n_sb, dh),
                          cache_sb_v[l].reshape(n_phys, page * n_sb, dh), page_table, bias_rows, n_sb, tq_pad)
        o_sb = o_sb.reshape(nbd, n_sb, tq_pad, dh)[:, :, :td].transpose(0, 2, 1, 3)
        o_sb = o_sb.reshape(1, n_tok_s, sbw).astype(BF16)
        pad_t = ((0, 0), (0, td_pad - td), (0, 0))
        qk_pad = jnp.pad(qk_r.reshape(nbd, td, 2 * rqw), pad_t)
        rest_pad = jnp.pad(rest.reshape(nbd, td, rest.shape[-1]), pad_t)
        o_norm, state_s = _retention(qk_pad, rest_pad, cos_s, sin_s, log_decay, state_ret[l], n_ret, dk, dv,
                                     td_pad, td)
        o_norm = o_norm[:, :td].reshape(1, n_tok_s, rvw)
        merged = _merge(o_sb, o_norm, rest, w_sb_up, w_ret_up)
        xs = _matmul_residual(merged, w_out, xs, mod_s[2])
        ks.append(k_sb.reshape(nbd, td, n_sb, dh))
        vs.append(v_sb.reshape(nbd, td, n_sb, dh))
        ss.append(state_s)
        h2_s, te_s, tw_s = _norm_router(xs, g_ffn[l], mod_s[3], mod_s[4], w_rt, b_router[l])

        top_e = jnp.concatenate([te_p.transpose(1, 0, 2).reshape(TOP_K, n_tok_p), te_s.reshape(TOP_K, n_tok_s),
                                 jnp.full((TOP_K, t_pad - n_tok), -1, I32)], axis=1)
        dest, info = _route(top_e, n_exp, moe_bm, n_blk)
        dest = dest.reshape(TOP_K, t_pad // DEST_TILE, DEST_TILE).transpose(1, 0, 2).reshape(-1, TOP_K * DEST_TILE)
        slots = _dispatch(dest, h2_p.reshape(n_tok_p, d), None, n_slot, 0)
        slots = _dispatch(dest, h2_s.reshape(n_tok_s, d), slots, n_slot, n_tok_p // DEST_TILE)
        yb = _moe_gemm(info, slots, w_gate_up[l], b_gate_up[l], w_down[l], b_down[l], moe_bm, n_blk)
        last = l == depth - 1
        xp = _combine(dest, tw_p.transpose(0, 2, 1), xp, mod_p[5], g_final, yb, 0, last)
        xs = _combine(dest, tw_s.transpose(0, 2, 1), xs, mod_s[5], g_final, yb, n_tok_p // DEST_TILE, last)

    return (xp, xs.reshape(nbd, td, d), jnp.stack(kp), jnp.stack(vp), jnp.stack(sp),
            jnp.stack(ks), jnp.stack(vs), jnp.stack(ss))
```

```python
import functools

import jax
import jax.numpy as jnp
from jax import lax
from jax.experimental import pallas as pl
from jax.experimental.pallas import tpu as pltpu

BF16 = jnp.bfloat16
F32 = jnp.float32
I32 = jnp.int32

NORM_EPS = 1e-5
ROPE_BASE = 10000.0
N_MOD = 6
TOP_K = 4
DEST_TILE = 128
SWIGLU_LIMIT = 7.0
SWIGLU_ALPHA = 1.702

V7X_VMEM_BYTES = 64 * 1024 * 1024
V7X_LANES = 128
V7X_SUBLANES = 8
BF16_ROWS = 2 * V7X_SUBLANES
VMEM_CEILING = V7X_VMEM_BYTES * 7 // 8


def _cparams(semantics, vmem_estimate):
    limit = min(max(int(vmem_estimate), 16 * 1024 * 1024), VMEM_CEILING)
    return pltpu.CompilerParams(dimension_semantics=semantics, vmem_limit_bytes=limit)


def _nbytes(shape, dtype):
    n = 1
    for s in shape:
        n *= s
    return n * jnp.dtype(dtype).itemsize


def _tile(n, want):
    t = min(n, want)
    while n % t:
        t -= 1
    return t


def _dot(a, b):
    return jnp.dot(a, b, preferred_element_type=F32)


def _dot_nt(a, b):
    return lax.dot_general(a, b, (((1,), (1,)), ((), ())), preferred_element_type=F32)


def _dot_tn(a, b):
    return lax.dot_general(a, b, (((0,), (0,)), ((), ())), preferred_element_type=F32)


def _sigmoid(x):
    return 1.0 / (1.0 + jnp.exp(-x))


LOG2_E = 1.4426950408889634


def _suffix_tri(n, with_row_sum):
    width = n + V7X_LANES if with_row_sum else n
    row = lax.broadcasted_iota(I32, (n, width), 0)
    col = lax.broadcasted_iota(I32, (n, width), 1)
    return jnp.where(jnp.logical_or(col >= n, row > col), 1.0, 0.0).astype(BF16)


def _ada_kernel(c_ref, w_ref, b_ref, o_ref):
    c = c_ref[...]
    s = (c * _sigmoid(c)).astype(BF16)
    o_ref[...] = _dot(s, w_ref[...].astype(BF16)) + b_ref[...]


def _adaln(c, w, b):
    m, d = c.shape
    n = w.shape[1]
    tn = _tile(n, 1024)
    est = 2 * (_nbytes((d, tn), F32) + _nbytes((m, tn), F32)) + _nbytes((d, tn), BF16) + _nbytes((m, d), F32) * 4
    return pl.pallas_call(
        _ada_kernel,
        grid=(n // tn,),
        in_specs=[pl.BlockSpec((m, d), lambda j: (0, 0)),
                  pl.BlockSpec((d, tn), lambda j: (0, j)),
                  pl.BlockSpec((1, tn), lambda j: (0, j))],
        out_specs=pl.BlockSpec((m, tn), lambda j: (0, j)),
        out_shape=jax.ShapeDtypeStruct((m, n), F32),
        compiler_params=_cparams(("parallel",), est + (8 << 20)),
        name="adaln",
    )(c, w, b.reshape(1, n))


def _normed(x_ref, g_ref, sh_ref, sc_ref):
    x = x_ref[0]
    y = x * lax.rsqrt(jnp.mean(x * x, axis=-1, keepdims=True) + NORM_EPS) * g_ref[...]
    return y * (1.0 + sc_ref[0]) + sh_ref[0]


def _norm_mod_kernel(x_ref, g_ref, sh_ref, sc_ref, h_ref):
    h_ref[0] = _normed(x_ref, g_ref, sh_ref, sc_ref).astype(h_ref.dtype)


def _norm_router_kernel(x_ref, g_ref, sh_ref, sc_ref, wr_ref, br_ref, h_ref, e_ref, p_ref):
    h = _normed(x_ref, g_ref, sh_ref, sc_ref)
    h_ref[0] = h
    logits = _dot_nt(wr_ref[...], h.astype(BF16)) + br_ref[...]
    n_exp = logits.shape[0]
    iota_e = lax.broadcasted_iota(I32, logits.shape, 0)
    tops = []
    for k in range(TOP_K):
        m = jnp.max(logits, axis=0, keepdims=True)
        idx = jnp.min(jnp.where(logits == m, iota_e, n_exp), axis=0, keepdims=True)
        e_ref[0, k:k + 1, :] = idx
        tops.append(m)
        logits = jnp.where(iota_e == idx, -jnp.inf, logits)
    ex = [jnp.exp(t - tops[0]) for t in tops]
    inv = 1.0 / (ex[0] + ex[1] + ex[2] + ex[3])
    for k in range(TOP_K):
        p_ref[0, k:k + 1, :] = ex[k] * inv


def _mod_spec(mod, ts, d):
    if mod.shape[1] == 1:
        return pl.BlockSpec((1, 1, d), lambda b, i: (b, 0, 0))
    return pl.BlockSpec((1, ts, d), lambda b, i: (b, i, 0))


def _norm_mod(x, g, sh, sc):
    nb, s, d = x.shape
    ts = _tile(s, 512)
    est = 2 * (_nbytes((ts, d), F32) + _nbytes((ts, d), BF16)) + 4 * _nbytes((ts, d), F32)
    return pl.pallas_call(
        _norm_mod_kernel,
        grid=(nb, s // ts),
        in_specs=[pl.BlockSpec((1, ts, d), lambda b, i: (b, i, 0)),
                  pl.BlockSpec((1, d), lambda b, i: (0, 0)),
                  _mod_spec(sh, ts, d), _mod_spec(sc, ts, d)],
        out_specs=pl.BlockSpec((1, ts, d), lambda b, i: (b, i, 0)),
        out_shape=jax.ShapeDtypeStruct((nb, s, d), BF16),
        compiler_params=_cparams(("parallel", "parallel"), est + (8 << 20)),
        name="norm_mod",
    )(x, g.reshape(1, d), sh, sc)


def _norm_router(x, g, sh, sc, w_router_t, b_router):
    nb, s, d = x.shape
    n_exp = w_router_t.shape[0]
    ts = _tile(s, 512)
    est = 4 * _nbytes((ts, d), F32) + 6 * _nbytes((ts, d), F32)
    return pl.pallas_call(
        _norm_router_kernel,
        grid=(nb, s // ts),
        in_specs=[pl.BlockSpec((1, ts, d), lambda b, i: (b, i, 0)),
                  pl.BlockSpec((1, d), lambda b, i: (0, 0)),
                  _mod_spec(sh, ts, d), _mod_spec(sc, ts, d),
                  pl.BlockSpec((n_exp, d), lambda b, i: (0, 0)),
                  pl.BlockSpec((n_exp, 1), lambda b, i: (0, 0))],
        out_specs=[pl.BlockSpec((1, ts, d), lambda b, i: (b, i, 0)),
                   pl.BlockSpec((1, TOP_K, ts), lambda b, i: (b, 0, i)),
                   pl.BlockSpec((1, TOP_K, ts), lambda b, i: (b, 0, i))],
        out_shape=[jax.ShapeDtypeStruct((nb, s, d), F32),
                   jax.ShapeDtypeStruct((nb, TOP_K, s), I32),
                   jax.ShapeDtypeStruct((nb, TOP_K, s), F32)],
        compiler_params=_cparams(("parallel", "parallel"), est + (8 << 20)),
        name="norm_router",
    )(x, g.reshape(1, d), sh, sc, w_router_t, b_router.reshape(n_exp, 1))


def _mm_kernel(a_ref, w_ref, *o_refs):
    acc = _dot(a_ref[0], w_ref[...])
    for o_ref in o_refs:
        o_ref[0] = acc.astype(o_ref.dtype)


def _mm_res_kernel(a_ref, w_ref, res_ref, gate_ref, o_ref):
    o_ref[0] = res_ref[0] + gate_ref[0] * _dot(a_ref[0], w_ref[...])


def _mm_tiles(s, k, n):
    return _tile(s, 1024), _tile(n, 1024)


def _matmul(a, w, *out_dtypes):
    nb, s, k = a.shape
    n = w.shape[1]
    tm, tn = _mm_tiles(s, k, n)
    est = 2 * (_nbytes((tm, k), BF16) + _nbytes((k, tn), BF16) + 2 * _nbytes((tm, tn), F32)) + _nbytes((tm, tn), F32)
    outs = pl.pallas_call(
        _mm_kernel,
        grid=(nb, s // tm, n // tn),
        in_specs=[pl.BlockSpec((1, tm, k), lambda b, i, j: (b, i, 0)),
                  pl.BlockSpec((k, tn), lambda b, i, j: (0, j))],
        out_specs=[pl.BlockSpec((1, tm, tn), lambda b, i, j: (b, i, j)) for _ in out_dtypes],
        out_shape=[jax.ShapeDtypeStruct((nb, s, n), dt) for dt in out_dtypes],
        compiler_params=_cparams(("parallel", "parallel", "parallel"), est + (8 << 20)),
        name="matmul",
    )(a, w)
    return outs[0] if len(outs) == 1 else outs


def _matmul_residual(a, w, res, gate):
    nb, s, k = a.shape
    n = w.shape[1]
    tm, tn = _mm_tiles(s, k, n)
    if gate.shape[1] == 1:
        gate_spec = pl.BlockSpec((1, 1, tn), lambda b, i, j: (b, 0, j))
    else:
        gate_spec = pl.BlockSpec((1, tm, tn), lambda b, i, j: (b, i, j))
    est = 2 * (_nbytes((tm, k), BF16) + _nbytes((k, tn), BF16) + 3 * _nbytes((tm, tn), F32)) + _nbytes((tm, tn), F32)
    return pl.pallas_call(
        _mm_res_kernel,
        grid=(nb, s // tm, n // tn),
        in_specs=[pl.BlockSpec((1, tm, k), lambda b, i, j: (b, i, 0)),
                  pl.BlockSpec((k, tn), lambda b, i, j: (0, j)),
                  pl.BlockSpec((1, tm, tn), lambda b, i, j: (b, i, j)),
                  gate_spec],
        out_specs=pl.BlockSpec((1, tm, tn), lambda b, i, j: (b, i, j)),
        out_shape=jax.ShapeDtypeStruct((nb, s, n), F32),
        compiler_params=_cparams(("parallel", "parallel", "parallel"), est + (8 << 20)),
        name="matmul_residual",
    )(a, w, res, gate)


def _sb_block(q, k, v, bias, scale, carry, tri, mask):
    a, carry = _sb_weights(_dot_nt(q, k) * scale + bias, carry, tri, mask)
    return _dot(a, v), carry


def _sb_weights(z2, carry, tri, mask):
    bk = z2.shape[1]
    soft = jnp.log2(1.0 + jnp.exp2(-jnp.abs(z2)))
    lb = jnp.minimum(z2, 0.0) - soft
    lk = lb - z2
    if mask is not None:
        lk = jnp.where(mask, lk, 0.0)
    cs = _dot(lk.astype(BF16), tri)
    row_sum = cs[:, bk:] if tri.shape[1] > bk else jnp.sum(lk, axis=1, keepdims=True)
    between = cs[:, :bk] + jnp.concatenate([carry] * (bk // V7X_LANES), axis=1)
    a = jnp.exp2(lb + between)
    if mask is not None:
        a = jnp.where(mask, a, 0.0)
    return a.astype(BF16), carry + row_sum


def _sbp_kernel(bias_ref, q_ref, k_ref, v_ref, o_ref, acc_ref, carry_ref, *, blk, scale, group, dh):
    qi = pl.program_id(2)
    biases = [bias_ref[pl.program_id(1) * group + g] * LOG2_E for g in range(group)]
    tri = _suffix_tri(blk, with_row_sum=False)
    row = lax.broadcasted_iota(I32, (blk, blk), 0)
    col = lax.broadcasted_iota(I32, (blk, blk), 1)

    def sweep(start, carries, mask):
        outs, new = [], []
        for g in range(group):
            cols = slice(g * dh, (g + 1) * dh)
            out, carry = _sb_block(q_ref[0, :, cols], k_ref[0, pl.ds(start, blk), cols],
                                   v_ref[0, pl.ds(start, blk), cols], biases[g], scale, carries[g], tri, mask)
            outs.append(out)
            new.append(carry)
        return jnp.concatenate(outs, axis=1), jnp.concatenate(new, axis=1)

    zero = jnp.zeros((blk, V7X_LANES), F32)
    out, carry = sweep(pl.multiple_of(qi * blk, blk), [zero] * group, col < row)
    acc_ref[...] = out
    carry_ref[...] = carry

    def body(j, _):
        carry = carry_ref[...]
        out, carry = sweep(pl.multiple_of((qi - 1 - j) * blk, blk),
                           [carry[:, g * V7X_LANES:(g + 1) * V7X_LANES] for g in range(group)], None)
        acc_ref[...] += out
        carry_ref[...] = carry
        return 0

    lax.fori_loop(0, qi, body, 0)
    o_ref[0] = acc_ref[...].astype(o_ref.dtype)


def _sb_prompt(q, k, v, bias, n_heads):
    nb, s, w = q.shape
    dh = w // n_heads
    blk = _tile(s, 512)
    group = _tile(n_heads, 8)
    gw = group * dh
    est = 8 * _nbytes((s, gw), BF16) + 4 * _nbytes((blk, gw), F32) + 12 * group * _nbytes((blk, blk), F32)
    return pl.pallas_call(
        functools.partial(_sbp_kernel, blk=blk, scale=dh ** -0.5 * LOG2_E, group=group, dh=dh),
        grid=(nb, n_heads // group, s // blk),
        in_specs=[pl.BlockSpec(memory_space=pltpu.SMEM),
                  pl.BlockSpec((1, blk, gw), lambda b, h, i: (b, i, h)),
                  pl.BlockSpec((1, s, gw), lambda b, h, i: (b, 0, h)),
                  pl.BlockSpec((1, s, gw), lambda b, h, i: (b, 0, h))],
        out_specs=pl.BlockSpec((1, blk, gw), lambda b, h, i: (b, i, h)),
        scratch_shapes=[pltpu.VMEM((blk, gw), F32), pltpu.VMEM((blk, group * V7X_LANES), F32)],
        out_shape=jax.ShapeDtypeStruct((nb, s, w), BF16),
        compiler_params=_cparams(("parallel", "parallel", "arbitrary"), est + (8 << 20)),
        name="sb_prompt",
    )(bias, q, k, v)


def _sbs_kernel(pt_ref, bias_ref, q_ref, kn_ref, vn_ref, *refs, pages_per_step, pages_per_group, rows_per_head,
                n_heads, page, scale):
    del pt_ref
    k_refs = refs[:pages_per_step]
    v_refs = refs[pages_per_step:2 * pages_per_step]
    o_ref, acc_ref, carry_ref = refs[2 * pages_per_step:]
    s = pl.program_id(1)
    rph = rows_per_head
    qs = [q_ref[0, h * rph:(h + 1) * rph, :] for h in range(n_heads)]
    bias = bias_ref[...] * LOG2_E

    def head_rows(page_refs, h):
        parts = [ref[0, pl.ds(h, page, stride=n_heads), :].astype(BF16) for ref in page_refs]
        return parts[0] if len(parts) == 1 else jnp.concatenate(parts, axis=0)

    def sweep(groups, mask):
        carry = carry_ref[...]
        total = None
        for k_pages, v_pages in groups:
            n = len(k_pages)
            z = jnp.concatenate([_dot_nt(qs[h], head_rows(k_pages, h)) for h in range(n_heads)], axis=0)
            z = z * scale + jnp.concatenate([bias] * n, axis=1)
            a, carry = _sb_weights(z, carry, _suffix_tri(n * page, with_row_sum=True), mask)
            out = jnp.concatenate([_dot(a[h * rph:(h + 1) * rph], head_rows(v_pages, h)) for h in range(n_heads)],
                                  axis=0)
            total = out if total is None else total + out
        acc_ref[...] += total
        carry_ref[...] = carry

    @pl.when(s == 0)
    def _():
        acc_ref[...] = jnp.zeros_like(acc_ref)
        carry_ref[...] = jnp.zeros_like(carry_ref)
        row = lax.broadcasted_iota(I32, (n_heads * rph, page), 0)
        col = lax.broadcasted_iota(I32, (n_heads * rph, page), 1)
        sweep([([kn_ref], [vn_ref])], col < row % rph)

    groups = [(list(k_refs[g:g + pages_per_group])[::-1], list(v_refs[g:g + pages_per_group])[::-1])
              for g in range(0, pages_per_step, pages_per_group)]
    sweep(groups, None)

    @pl.when(s == pl.num_programs(1) - 1)
    def _():
        o_ref[0] = acc_ref[...]


def _sb_sample(q, k_new, v_new, cache_k, cache_v, page_table, bias_rows, n_heads, rows_per_head):
    nbd, r, dh = q.shape
    page = cache_k.shape[1] // n_heads
    n_pages = page_table.shape[1]
    pps = _tile(n_pages, 8)
    n_steps = n_pages // pps
    blk = (1, page * n_heads, dh)

    def page_spec(p):
        return pl.BlockSpec(blk, lambda b, s, pt: (pt[b, n_pages - 1 - (s * pps + p)], 0, 0))

    in_specs = [pl.BlockSpec((r, V7X_LANES), lambda b, s, pt: (0, 0)),
                pl.BlockSpec((1, r, dh), lambda b, s, pt: (b, 0, 0)),
                pl.BlockSpec(blk, lambda b, s, pt: (b, 0, 0)),
                pl.BlockSpec(blk, lambda b, s, pt: (b, 0, 0))]
    in_specs += [page_spec(p) for p in range(pps)] * 2
    est = 2 * (2 * pps + 2) * _nbytes(blk, F32) + 16 * _nbytes((r, page), F32) + 4 * _nbytes(blk, BF16)
    return pl.pallas_call(
        functools.partial(_sbs_kernel, pages_per_step=pps, pages_per_group=_tile(pps, 2),
                          rows_per_head=rows_per_head, n_heads=n_heads,
                          page=page, scale=dh ** -0.5 * LOG2_E),
        grid_spec=pltpu.PrefetchScalarGridSpec(
            num_scalar_prefetch=1,
            grid=(nbd, n_steps),
            in_specs=in_specs,
            out_specs=pl.BlockSpec((1, r, dh), lambda b, s, pt: (b, 0, 0)),
            scratch_shapes=[pltpu.VMEM((r, dh), F32), pltpu.VMEM((r, V7X_LANES), F32)]),
        out_shape=jax.ShapeDtypeStruct((nbd, r, dh), F32),
        compiler_params=_cparams(("parallel", "arbitrary"), est + (8 << 20)),
        name="sb_sample",
    )(page_table, bias_rows, q, k_new, v_new, *([cache_k] * pps), *([cache_v] * pps))


def _ret_kernel(lg_ref, q_ref, k_ref, v_ref, cos_ref, sin_ref, *refs, c_true, has_init, kscale, n_heads, dv):
    if has_init:
        s0_ref, o_ref, sout_ref, st_ref = refs
    else:
        o_ref, sout_ref, st_ref = refs
    c = pl.program_id(1)

    @pl.when(c == 0)
    def _():
        if has_init:
            st_ref[...] = s0_ref[0]
        else:
            st_ref[...] = jnp.zeros_like(st_ref)

    cos = cos_ref[...]
    sin = sin_ref[...]
    n_rows, dk = cos.shape

    def rot(x):
        return x * cos + pltpu.roll(x, dk // 2, 1) * sin

    n = lax.broadcasted_iota(I32, (n_rows, n_rows), 0)
    m = lax.broadcasted_iota(I32, (n_rows, n_rows), 1)
    diff = (n - m).astype(F32)
    pos = lax.broadcasted_iota(I32, (n_rows, dk), 0).astype(F32)
    outs = []
    for h in range(n_heads):
        lg = lg_ref[h]
        q = rot(q_ref[0, :, h * dk:(h + 1) * dk])
        k = rot(k_ref[0, :, h * dk:(h + 1) * dk]) * kscale
        v = v_ref[0, :, h * dv:(h + 1) * dv]
        decay = jnp.where(diff >= 0.0, jnp.exp(jnp.maximum(diff, 0.0) * lg), 0.0)
        scores = _dot_nt(q.astype(BF16), k.astype(BF16)) * decay
        q_dec = q * jnp.exp((pos + 1.0) * lg)
        k_dec = k * jnp.exp((c_true - 1.0 - pos) * lg)
        st = st_ref[h]
        o = _dot(scores.astype(BF16), v) + _dot(q_dec.astype(BF16), st.astype(BF16))
        st_ref[h] = jnp.exp(jnp.full((1, dv), c_true * lg, F32)) * st + _dot_tn(k_dec.astype(BF16), v)
        mu = jnp.mean(o, axis=-1, keepdims=True)
        dev = o - mu
        var = jnp.mean(dev * dev, axis=-1, keepdims=True)
        outs.append((dev * lax.rsqrt(var + NORM_EPS)).astype(o_ref.dtype))
    o_ref[0] = jnp.concatenate(outs, axis=1)

    @pl.when(c == pl.num_programs(1) - 1)
    def _():
        sout_ref[0] = st_ref[...]


def _retention(qk, v_src, cos2, sin2, log_decay, state0, n_heads, dk, dv, chunk, c_true):
    nb, s, _ = qk.shape
    nc = s // chunk
    qw, vw = n_heads * dk, n_heads * dv
    in_specs = [pl.BlockSpec(memory_space=pltpu.SMEM),
                pl.BlockSpec((1, chunk, qw), lambda b, c: (b, c, 0)),
                pl.BlockSpec((1, chunk, qw), lambda b, c: (b, c, 1)),
                pl.BlockSpec((1, chunk, vw), lambda b, c: (b, c, 0)),
                pl.BlockSpec((chunk, dk), lambda b, c: (c, 0)),
                pl.BlockSpec((chunk, dk), lambda b, c: (c, 0))]
    args = [log_decay, qk, qk, v_src, cos2, sin2]
    if state0 is not None:
        in_specs.append(pl.BlockSpec((1, n_heads, dk, dv), lambda b, c: (b, 0, 0, 0)))
        args.append(state0)
    est = (7 * _nbytes((n_heads, dk, dv), F32) + 4 * _nbytes((chunk, qw), F32) + 4 * _nbytes((chunk, vw), BF16)
           + n_heads * (8 * _nbytes((chunk, dv), F32) + 6 * _nbytes((chunk, chunk), F32)))
    return pl.pallas_call(
        functools.partial(_ret_kernel, c_true=float(c_true), has_init=state0 is not None, kscale=dk ** -0.5,
                          n_heads=n_heads, dv=dv),
        grid=(nb, nc),
        in_specs=in_specs,
        out_specs=[pl.BlockSpec((1, chunk, vw), lambda b, c: (b, c, 0)),
                   pl.BlockSpec((1, n_heads, dk, dv), lambda b, c: (b, 0, 0, 0))],
        out_shape=[jax.ShapeDtypeStruct((nb, s, vw), BF16),
                   jax.ShapeDtypeStruct((nb, n_heads, dk, dv), F32)],
        scratch_shapes=[pltpu.VMEM((n_heads, dk, dv), F32)],
        compiler_params=_cparams(("parallel", "arbitrary"), est + (8 << 20)),
        name="retention",
    )(*args)


def _merge_kernel(osb_ref, on_ref, g_ref, asb_ref, ar_ref, wsb_ref, wret_ref, o_ref):
    y_sb = _dot(osb_ref[0], wsb_ref[...])
    g = g_ref[0].astype(F32)
    u = (g * _sigmoid(g)) * on_ref[0].astype(F32)
    y_r = _dot(u.astype(BF16), wret_ref[...])
    merged = _sigmoid(asb_ref[0].astype(F32)) * y_sb + _sigmoid(ar_ref[0].astype(F32)) * y_r
    o_ref[0] = merged.astype(o_ref.dtype)


def _merge(o_sb, o_norm, rest, w_up_sb, w_up_ret):
    nb, s, sbw = o_sb.shape
    rvw = o_norm.shape[2]
    d = w_up_sb.shape[1]
    tm = _tile(s, 256)
    est = (2 * (_nbytes((sbw, d), BF16) + _nbytes((rvw, d), BF16))
           + 2 * (_nbytes((tm, sbw), BF16) + 5 * _nbytes((tm, d), BF16)) + 8 * _nbytes((tm, d), F32))
    return pl.pallas_call(
        _merge_kernel,
        grid=(nb, s // tm),
        in_specs=[pl.BlockSpec((1, tm, sbw), lambda b, i: (b, i, 0)),
                  pl.BlockSpec((1, tm, rvw), lambda b, i: (b, i, 0)),
                  pl.BlockSpec((1, tm, rvw), lambda b, i: (b, i, 1)),
                  pl.BlockSpec((1, tm, d), lambda b, i: (b, i, 2)),
                  pl.BlockSpec((1, tm, d), lambda b, i: (b, i, 3)),
                  pl.BlockSpec((sbw, d), lambda b, i: (0, 0)),
                  pl.BlockSpec((rvw, d), lambda b, i: (0, 0))],
        out_specs=pl.BlockSpec((1, tm, d), lambda b, i: (b, i, 0)),
        out_shape=jax.ShapeDtypeStruct((nb, s, d), BF16),
        compiler_params=_cparams(("parallel", "parallel"), est + (8 << 20)),
        name="merge",
    )(o_sb, o_norm, rest, rest, rest, w_up_sb, w_up_ret)


def _route_kernel(te_ref, dest_ref, info_ref, cnt_ref, base_ref, *, n_exp, bm, n_info):
    phase = pl.program_id(0)
    i = pl.program_id(1)
    te = te_ref[...]
    tm = te.shape[1]
    iota_e = lax.broadcasted_iota(I32, (n_exp, tm), 0)
    ones = jnp.ones((tm, V7X_LANES), BF16)
    onehots = [jnp.where(iota_e == te[k:k + 1, :], 1.0, 0.0).astype(BF16) for k in range(TOP_K)]

    @pl.when(jnp.logical_and(phase == 0, i == 0))
    def _():
        cnt_ref[...] = jnp.zeros_like(cnt_ref)

    @pl.when(phase == 0)
    def _():
        cnt_ref[...] += _dot(onehots[0] + onehots[1] + onehots[2] + onehots[3], ones)

    @pl.when(jnp.logical_and(phase == 1, i == 0))
    def _():
        counts = cnt_ref[...]
        n_blocks = jnp.floor((counts + (bm - 0.5)) * (1.0 / bm))
        padded = n_blocks * bm
        er = lax.broadcasted_iota(I32, (n_exp, n_exp), 0)
        ec = lax.broadcasted_iota(I32, (n_exp, n_exp), 1)
        lower = jnp.where(ec < er, 1.0, 0.0).astype(BF16)
        hi = jnp.floor(n_blocks * (1.0 / 256.0))
        lo = n_blocks - 256.0 * hi
        pstart = (256.0 * _dot(lower, hi.astype(BF16)) + _dot(lower, lo.astype(BF16))) * bm
        base_ref[...] = pstart
        reps = n_info // V7X_LANES
        pend = jnp.concatenate([pstart + padded] * reps, axis=1)
        used = jnp.concatenate([pstart + counts] * reps, axis=1)
        blk_start = (lax.broadcasted_iota(I32, (n_exp, n_info), 1) * bm).astype(F32)
        blk_e = jnp.minimum(jnp.sum(jnp.where(pend <= blk_start, 1.0, 0.0), axis=0, keepdims=True), n_exp - 1.0)
        mine = lax.broadcasted_iota(I32, (n_exp, n_info), 0).astype(F32) == blk_e
        valid = jnp.sum(jnp.where(mine, used - blk_start, 0.0), axis=0, keepdims=True)
        valid = jnp.clip(valid, 0.0, float(bm))
        n_active = jnp.floor((jnp.sum(padded[:, :1], axis=0, keepdims=True) + 0.5) * (1.0 / bm))
        info_ref[...] = jnp.zeros_like(info_ref)
        info_ref[0:1, :] = blk_e.astype(I32)
        info_ref[1:2, :] = valid.astype(I32)
        info_ref[2:3, :] = jnp.broadcast_to(n_active, (1, n_info)).astype(I32)

    @pl.when(phase == 1)
    def _():
        tri = jnp.where(lax.broadcasted_iota(I32, (tm, tm), 0) < lax.broadcasted_iota(I32, (tm, tm), 1),
                        1.0, 0.0).astype(BF16)
        base = base_ref[...]
        for k in range(TOP_K):
            before = _dot(onehots[k], tri) + jnp.concatenate([base] * (tm // V7X_LANES), axis=1)
            dest = jnp.sum(jnp.where(onehots[k] > 0, before, 0.0), axis=0, keepdims=True)
            dest_ref[k:k + 1, :] = dest.astype(I32)
            base = base + _dot(onehots[k], ones)
        base_ref[...] = base


def _route(top_e, n_exp, bm, n_blk):
    _, tpad = top_e.shape
    tm = _tile(tpad, 512)
    n_info = -(-n_blk // V7X_LANES) * V7X_LANES
    est = 8 * _nbytes((tm, tm), F32) + 16 * _nbytes((n_exp, tm), F32)
    return pl.pallas_call(
        functools.partial(_route_kernel, n_exp=n_exp, bm=bm, n_info=n_info),
        grid=(2, tpad // tm),
        in_specs=[pl.BlockSpec((TOP_K, tm), lambda p, i: (0, i))],
        out_specs=[pl.BlockSpec((TOP_K, tm), lambda p, i: (0, i * p)),
                   pl.BlockSpec((V7X_SUBLANES, n_info), lambda p, i: (0, 0))],
        out_shape=[jax.ShapeDtypeStruct((TOP_K, tpad), I32),
                   jax.ShapeDtypeStruct((V7X_SUBLANES, n_info), I32)],
        scratch_shapes=[pltpu.VMEM((n_exp, V7X_LANES), F32), pltpu.VMEM((n_exp, V7X_LANES), F32)],
        compiler_params=_cparams(("arbitrary", "arbitrary"), est + (8 << 20)),
        name="moe_route",
    )(top_e)


def _row_copy(src_ref, src_row, dst_ref, dst_row, sem):
    return pltpu.make_async_copy(src_ref.at[pl.ds(src_row, 1)], dst_ref.at[pl.ds(dst_row, 1)], sem)


def _dispatch_kernel(dest_ref, h_ref, *refs, rows, first_tile):
    xs_ref, sem = refs[-2:]
    tile = first_tile + pl.program_id(0)

    def issue(r, _):
        for k in range(TOP_K):
            _row_copy(h_ref, r, xs_ref, dest_ref[tile, k * rows + r], sem).start()
        return 0

    lax.fori_loop(0, rows, issue, 0, unroll=8)

    def drain(r, _):
        for k in range(TOP_K):
            _row_copy(h_ref, 0, xs_ref, 0, sem).wait()
        return 0

    lax.fori_loop(0, rows, drain, 0, unroll=True)


def _dispatch(dest, h, xs, n_slot, first_tile):
    t, d = h.shape
    rows = DEST_TILE
    assert t % rows == 0
    in_specs = [pl.BlockSpec((rows, d), lambda i, dest: (i, 0))]
    args = [dest, h]
    aliases = {}
    if xs is not None:
        in_specs.append(pl.BlockSpec(memory_space=pl.ANY))
        args.append(xs)
        aliases = {2: 0}
    return pl.pallas_call(
        functools.partial(_dispatch_kernel, rows=rows, first_tile=first_tile),
        grid_spec=pltpu.PrefetchScalarGridSpec(
            num_scalar_prefetch=1,
            grid=(t // rows,),
            in_specs=in_specs,
            out_specs=pl.BlockSpec(memory_space=pl.ANY),
            scratch_shapes=[pltpu.SemaphoreType.DMA]),
        out_shape=jax.ShapeDtypeStruct((n_slot, d), F32),
        input_output_aliases=aliases,
        compiler_params=_cparams(("arbitrary",), 4 * _nbytes((rows, d), F32) + (8 << 20)),
        name="moe_dispatch",
    )(*args)


def _moe_gemm_kernel(info_ref, x_ref, wg_ref, wu_ref, bg_ref, bu_ref, wd_ref, bd_ref, o_ref, xb_ref):
    i = pl.program_id(0)
    j = pl.program_id(1)

    @pl.when(i < info_ref[2, 0])
    def _():
        @pl.when(j == 0)
        def _():
            row = lax.broadcasted_iota(I32, x_ref.shape, 0)
            xb_ref[...] = jnp.where(row < info_ref[1, i], x_ref[...], 0.0).astype(BF16)
            o_ref[...] = jnp.broadcast_to(bd_ref[0], o_ref.shape)

        xb = xb_ref[...]
        gate = jnp.minimum(_dot(xb, wg_ref[0].astype(BF16)) + bg_ref[0], SWIGLU_LIMIT)
        up = jnp.clip(_dot(xb, wu_ref[0].astype(BF16)) + bu_ref[0], -SWIGLU_LIMIT, SWIGLU_LIMIT)
        act = (up + 1.0) * gate * _sigmoid(SWIGLU_ALPHA * gate)
        o_ref[...] += _dot(act.astype(BF16), wd_ref[0].astype(BF16))


def _moe_gemm(info, xs, w_gate_up, b_gate_up, w_down, b_down, bm, n_blk):
    n_slot, d = xs.shape
    n_exp, _, f2 = w_gate_up.shape
    f = f2 // 2
    tf = _tile(f, 512)
    nf = f // tf

    def blk(i, info):
        return jnp.minimum(i, info[2, 0] - 1)

    def ff(i, j, info):
        return jnp.where(i < info[2, 0], j, nf - 1)

    in_specs = [pl.BlockSpec((bm, d), lambda i, j, info: (blk(i, info), 0)),
                pl.BlockSpec((1, d, tf), lambda i, j, info: (info[0, blk(i, info)], 0, ff(i, j, info))),
                pl.BlockSpec((1, d, tf), lambda i, j, info: (info[0, blk(i, info)], 0, nf + ff(i, j, info))),
                pl.BlockSpec((1, 1, tf), lambda i, j, info: (info[0, blk(i, info)], 0, ff(i, j, info))),
                pl.BlockSpec((1, 1, tf), lambda i, j, info: (info[0, blk(i, info)], 0, nf + ff(i, j, info))),
                pl.BlockSpec((1, tf, d), lambda i, j, info: (info[0, blk(i, info)], ff(i, j, info), 0)),
                pl.BlockSpec((1, 1, d), lambda i, j, info: (info[0, blk(i, info)], 0, 0))]
    est = (2 * (2 * _nbytes((bm, d), F32) + 3 * _nbytes((d, tf), F32)) + _nbytes((bm, d), BF16)
           + 3 * _nbytes((d, tf), BF16) + 6 * _nbytes((bm, tf), F32))
    return pl.pallas_call(
        _moe_gemm_kernel,
        grid_spec=pltpu.PrefetchScalarGridSpec(
            num_scalar_prefetch=1,
            grid=(n_blk, nf),
            in_specs=in_specs,
            out_specs=pl.BlockSpec((bm, d), lambda i, j, info: (blk(i, info), 0)),
            scratch_shapes=[pltpu.VMEM((bm, d), BF16)]),
        out_shape=jax.ShapeDtypeStruct((n_slot, d), F32),
        compiler_params=_cparams(("arbitrary", "arbitrary"), est + (8 << 20)),
        name="moe_gemm",
    )(info, xs, w_gate_up, w_gate_up, b_gate_up.reshape(n_exp, 1, f2), b_gate_up.reshape(n_exp, 1, f2),
      w_down, b_down.reshape(n_exp, 1, d))


def _combine_kernel(dest_ref, p_ref, res_ref, gate_ref, g_ref, yb_ref, o_ref, buf_ref, sem, *, rows, first_tile,
                    final_norm):
    tile = first_tile + pl.program_id(0) * pl.num_programs(1) + pl.program_id(1)

    def issue(r, _):
        for k in range(TOP_K):
            _row_copy(yb_ref, dest_ref[tile, k * rows + r], buf_ref.at[k], r, sem).start()
        return 0

    lax.fori_loop(0, rows, issue, 0, unroll=8)

    def drain(r, _):
        for k in range(TOP_K):
            _row_copy(yb_ref, 0, buf_ref.at[k], 0, sem).wait()
        return 0

    lax.fori_loop(0, rows, drain, 0, unroll=True)
    p = p_ref[0]
    y = p[:, 0:1] * buf_ref[0]
    for k in range(1, TOP_K):
        y = y + p[:, k:k + 1] * buf_ref[k]
    x = res_ref[0] + gate_ref[0] * y
    if final_norm:
        x = x * lax.rsqrt(jnp.mean(x * x, axis=-1, keepdims=True) + NORM_EPS) * g_ref[...]
    o_ref[0] = x


def _combine(dest, probs, res, gate, g_final, yb, first_tile, final_norm):
    nb, s, d = res.shape
    rows = DEST_TILE
    assert s % rows == 0
    if gate.shape[1] == 1:
        gate_spec = pl.BlockSpec((1, 1, d), lambda b, i, dest: (b, 0, 0))
    else:
        gate_spec = pl.BlockSpec((1, rows, d), lambda b, i, dest: (b, i, 0))
    est = _nbytes((TOP_K, rows, d), F32) + 6 * _nbytes((rows, d), F32) + 4 * _nbytes((rows, d), F32)
    return pl.pallas_call(
        functools.partial(_combine_kernel, rows=rows, first_tile=first_tile, final_norm=final_norm),
        grid_spec=pltpu.PrefetchScalarGridSpec(
            num_scalar_prefetch=1,
            grid=(nb, s // rows),
            in_specs=[pl.BlockSpec((1, rows, TOP_K), lambda b, i, dest: (b, i, 0)),
                      pl.BlockSpec((1, rows, d), lambda b, i, dest: (b, i, 0)),
                      gate_spec,
                      pl.BlockSpec((1, d), lambda b, i, dest: (0, 0)),
                      pl.BlockSpec(memory_space=pl.ANY)],
            out_specs=pl.BlockSpec((1, rows, d), lambda b, i, dest: (b, i, 0)),
            scratch_shapes=[pltpu.VMEM((TOP_K, rows, d), F32), pltpu.SemaphoreType.DMA]),
        out_shape=jax.ShapeDtypeStruct((nb, s, d), F32),
        compiler_params=_cparams(("arbitrary", "arbitrary"), est + (8 << 20)),
        name="moe_combine",
    )(dest, probs, res, gate, g_final.reshape(1, d), yb)


def _rotary_tables(pos, dk):
    half = dk // 2
    inv_freq = ROPE_BASE ** (-jnp.arange(half, dtype=F32) / half)
    ang = pos.astype(F32)[:, None] * inv_freq[None, :]
    cos, sin = jnp.cos(ang), jnp.sin(ang)
    return jnp.concatenate([cos, cos], axis=1), jnp.concatenate([-sin, sin], axis=1)


def kernel(x_prompt, x_sample, cache_sb_k, cache_sb_v, state_ret, page_table, c_prompt, c_sample,
           w_ada, b_ada, g_mix, w_in, b_sb, w_up_sb, w_up_ret, w_o, g_ffn, w_router, b_router,
           w_gate_up, b_gate_up, w_down, b_down, g_final):
    nb, s, d = x_prompt.shape
    nbd, td, _ = x_sample.shape
    depth, n_phys, page, n_sb, dh = cache_sb_k.shape
    _, _, n_ret, dk, dv = state_ret.shape
    n_pages = page_table.shape[1]
    past = n_pages * page
    sbw, rqw, rvw = n_sb * dh, n_ret * dk, n_ret * dv
    n_exp = w_router.shape[-1]
    assert rvw == d and w_in.shape[-1] == 3 * sbw + 2 * rqw + 2 * rvw + 2 * d
    n_tok_p, n_tok_s = nb * s, nbd * td
    n_tok = n_tok_p + n_tok_s

    ret_chunk = _tile(s, 128)
    td_pad = -(-td // V7X_SUBLANES) * V7X_SUBLANES
    tq_pad = -(-td // BF16_ROWS) * BF16_ROWS
    cos_p, sin_p = _rotary_tables(jnp.arange(s), dk)
    cos_s, sin_s = _rotary_tables(past + jnp.arange(td_pad), dk)
    log_decay = jnp.log1p(-jnp.exp2(-5.0 - jnp.arange(n_ret, dtype=F32)))

    moe_bm = 768
    n_assign = n_tok * TOP_K
    n_blk = -(-n_assign // moe_bm) + n_exp
    n_slot = n_blk * moe_bm
    route_tile = 512
    t_pad = -(-n_tok // route_tile) * route_tile

    xp = x_prompt
    xs = x_sample.reshape(1, n_tok_s, d)
    c_rows = -(-(nb + nbd) // 16) * 16
    c_all = jnp.concatenate([c_prompt, c_sample, jnp.zeros((c_rows - nb - nbd, d), F32)], axis=0)
    kp, vp, sp, ks, vs, ss = [], [], [], [], [], []

    for l in range(depth):
        mods = jnp.split(_adaln(c_all, w_ada[l], b_ada[l]), N_MOD, axis=1)
        mod_p = [m[:nb].reshape(nb, 1, d) for m in mods]
        mod_s = [jnp.repeat(m[nb:nb + nbd], td, axis=0).reshape(1, n_tok_s, d) for m in mods]

        w = w_in[l]
        o1, o2, o3, o4 = sbw, 2 * sbw, 3 * sbw, 3 * sbw + 2 * rqw
        w_q, w_k, w_v = (w[:, a:b].astype(BF16) for a, b in ((0, o1), (o1, o2), (o2, o3)))
        w_qk_r = w[:, o3:o4].astype(BF16)
        w_rest = w[:, o4:].astype(BF16)
        w_sb_up, w_ret_up, w_out = w_up_sb[l].astype(BF16), w_up_ret[l].astype(BF16), w_o[l].astype(BF16)
        w_rt = w_router[l].T.astype(BF16)

        def in_proj(h):
            return (_matmul(h, w_q, BF16), _matmul(h, w_k, F32, BF16), _matmul(h, w_v, F32, BF16),
                    _matmul(h, w_qk_r, F32), _matmul(h, w_rest, BF16))

        h = _norm_mod(xp, g_mix[l], mod_p[0], mod_p[1])
        q_sb, (k_sb, k_bf), (v_sb, v_bf), qk_r, rest = in_proj(h)
        o_sb = _sb_prompt(q_sb, k_bf, v_bf, b_sb[l], n_sb)
        o_norm, state_p = _retention(qk_r, rest, cos_p, sin_p, log_decay, None, n_ret, dk, dv, ret_chunk, ret_chunk)
        merged = _merge(o_sb, o_norm, rest, w_sb_up, w_ret_up)
        xp = _matmul_residual(merged, w_out, xp, mod_p[2])
        kp.append(k_sb.reshape(nb, s, n_sb, dh))
        vp.append(v_sb.reshape(nb, s, n_sb, dh))
        sp.append(state_p)
        h2_p, te_p, tw_p = _norm_router(xp, g_ffn[l], mod_p[3], mod_p[4], w_rt, b_router[l])

        h = _norm_mod(xs, g_mix[l], mod_s[0], mod_s[1])
        q_sb, (k_sb, _), (v_sb, _), qk_r, rest = in_proj(h)
        q_rows = jnp.pad(q_sb.reshape(nbd, td, n_sb, dh).transpose(0, 2, 1, 3),
                         ((0, 0), (0, 0), (0, tq_pad - td), (0, 0))).reshape(nbd, n_sb * tq_pad, dh)
        k_new = jnp.pad(k_sb.reshape(nbd, td, sbw), ((0, 0), (0, page - td), (0, 0))).reshape(nbd, page * n_sb, dh)
        v_new = jnp.pad(v_sb.reshape(nbd, td, sbw), ((0, 0), (0, page - td), (0, 0))).reshape(nbd, page * n_sb, dh)
        bias_rows = jnp.broadcast_to(jnp.repeat(b_sb[l], tq_pad)[:, None], (n_sb * tq_pad, V7X_LANES))
        o_sb = _sb_sample(q_rows, k_new, v_new, cache_sb_k[l].reshape(n_phys, page * n_sb, dh),
                          cache_sb_v[l].reshape(n_phys, page * n_sb, dh), page_table, bias_rows, n_sb, tq_pad)
        o_sb = o_sb.reshape(nbd, n_sb, tq_pad, dh)[:, :, :td].transpose(0, 2, 1, 3)
        o_sb = o_sb.reshape(1, n_tok_s, sbw).astype(BF16)
        pad_t = ((0, 0), (0, td_pad - td), (0, 0))
        qk_pad = jnp.pad(qk_r.reshape(nbd, td, 2 * rqw), pad_t)
        rest_pad = jnp.pad(rest.reshape(nbd, td, rest.shape[-1]), pad_t)
        o_norm, state_s = _retention(qk_pad, rest_pad, cos_s, sin_s, log_decay, state_ret[l], n_ret, dk, dv,
                                     td_pad, td)
        o_norm = o_norm[:, :td].reshape(1, n_tok_s, rvw)
        merged = _merge(o_sb, o_norm, rest, w_sb_up, w_ret_up)
        xs = _matmul_residual(merged, w_out, xs, mod_s[2])
        ks.append(k_sb.reshape(nbd, td, n_sb, dh))
        vs.append(v_sb.reshape(nbd, td, n_sb, dh))
        ss.append(state_s)
        h2_s, te_s, tw_s = _norm_router(xs, g_ffn[l], mod_s[3], mod_s[4], w_rt, b_router[l])

        top_e = jnp.concatenate([te_p.transpose(1, 0, 2).reshape(TOP_K, n_tok_p), te_s.reshape(TOP_K, n_tok_s),
                                 jnp.full((TOP_K, t_pad - n_tok), -1, I32)], axis=1)
        dest, info = _route(top_e, n_exp, moe_bm, n_blk)
        dest = dest.reshape(TOP_K, t_pad // DEST_TILE, DEST_TILE).transpose(1, 0, 2).reshape(-1, TOP_K * DEST_TILE)
        slots = _dispatch(dest, h2_p.reshape(n_tok_p, d), None, n_slot, 0)
        slots = _dispatch(dest, h2_s.reshape(n_tok_s, d), slots, n_slot, n_tok_p // DEST_TILE)
        yb = _moe_gemm(info, slots, w_gate_up[l], b_gate_up[l], w_down[l], b_down[l], moe_bm, n_blk)
        last = l == depth - 1
        xp = _combine(dest, tw_p.transpose(0, 2, 1), xp, mod_p[5], g_final, yb, 0, last)
        xs = _combine(dest, tw_s.transpose(0, 2, 1), xs, mod_s[5], g_final, yb, n_tok_p // DEST_TILE, last)

    return (xp, xs.reshape(nbd, td, d), jnp.stack(kp), jnp.stack(vp), jnp.stack(sp),
            jnp.stack(ks), jnp.stack(vs), jnp.stack(ss))
```
